```python
import math
import jax, jax.numpy as jnp
from jax import lax
import numpy as np

D_MODEL = 2048
BATCH = 2
SEQ = 4096
DEPTH = 4
DEC_BATCH = 8
DEC_SEQ = 1
PAST_LEN = 16384
PAGE_SIZE = 128

W_SSM = D_MODEL // 4
SSM_P = 16
SSM_G = W_SSM // SSM_P
SSM_N = 64
DIFF_HD = 128
DIFF_HALF = DIFF_HD // 2
DIFF_H = (3 * D_MODEL // 8) // DIFF_HD
W_DIFF = DIFF_H * DIFF_HD
SB_HD = 128
SB_H = (D_MODEL - W_SSM - W_DIFF) // SB_HD
W_SB = SB_H * SB_HD
W_MIX = W_SSM + W_DIFF + W_SB
IN_COLS = 2 * W_SSM + 4 * W_DIFF + 4 * W_SB
Q_BLOCK = 128
EPS = 1e-6

kernel_name = 'hymba_s5_diffattn_stickbreaking_step'


def _rms(x, g):
    xf = x.astype(jnp.float32)
    y = xf * lax.rsqrt(jnp.mean(xf * xf, axis=-1, keepdims=True) + EPS)
    return (y * g.astype(jnp.float32)).astype(x.dtype)


def _sweep(block_fn, q, q_offset):
    b, tq = q.shape[0], q.shape[1]
    if tq <= Q_BLOCK:
        return block_fn(q, jnp.asarray(q_offset, jnp.int32))
    nb = tq // Q_BLOCK
    qb = jnp.moveaxis(q.reshape((b, nb, Q_BLOCK) + q.shape[2:]), 1, 0)
    starts = q_offset + Q_BLOCK * jnp.arange(nb, dtype=jnp.int32)
    out = lax.map(lambda a: block_fn(a[0], a[1]), (qb, starts))
    out = jnp.moveaxis(out, 0, 1)
    return out.reshape((b, tq) + out.shape[3:])


def _diff_attention(q, k, v, lam):
    tq, tk = q.shape[1], k.shape[1]
    kpos = jnp.arange(tk)
    scale = DIFF_HALF ** -0.5

    def block(qb, start):
        qpos = start + jnp.arange(qb.shape[1])
        s = jnp.einsum('bqhcd,bkhcd->bhcqk', qb, k, preferred_element_type=jnp.float32) * scale
        s = jnp.where(kpos[None, :] <= qpos[:, None], s, -jnp.inf)
        p = jax.nn.softmax(s, axis=-1)
        w = p[:, :, 0] - lam * p[:, :, 1]
        return jnp.einsum('bhqk,bkhd->bqhd', w.astype(v.dtype), v)

    return _sweep(block, q, tk - tq)


def _stick_breaking(q, k, v):
    tq, tk = q.shape[1], k.shape[1]
    kpos = jnp.arange(tk)
    scale = SB_HD ** -0.5

    def block(qb, start):
        qpos = start + jnp.arange(qb.shape[1])
        z = jnp.einsum('bqhd,bkhd->bhqk', qb, k, preferred_element_type=jnp.float32) * scale
        mask = kpos[None, :] < qpos[:, None]
        log_beta = jax.nn.log_sigmoid(z)
        log_keep = jnp.where(mask, jax.nn.log_sigmoid(-z), 0.0)
        later = lax.cumsum(log_keep, axis=3, reverse=True) - log_keep
        a = jnp.where(mask, jnp.exp(log_beta + later), 0.0)
        return jnp.einsum('bhqk,bkhd->bqhd', a.astype(v.dtype), v)

    return _sweep(block, q, tk - tq)


def _s5(u, h0_re, h0_im, lam_re, lam_im, log_dt, b_re, b_im, c_re, c_im, d_skip):
    f32 = jnp.float32
    bsz, t = u.shape[0], u.shape[1]
    uf = u.astype(f32).reshape(bsz, t, SSM_G, SSM_P)
    lr, li = lam_re.astype(f32), lam_im.astype(f32)
    dt = jnp.exp(log_dt.astype(f32))[:, None]
    mag = jnp.exp(lr * dt)
    ar, ai = mag * jnp.cos(li * dt), mag * jnp.sin(li * dt)
    den = lr * lr + li * li
    kr = ((ar - 1.0) * lr + ai * li) / den
    ki = (ai * lr - (ar - 1.0) * li) / den
    br, bi = b_re.astype(f32), b_im.astype(f32)
    bbar_re = kr[..., None] * br - ki[..., None] * bi
    bbar_im = kr[..., None] * bi + ki[..., None] * br
    xr = jnp.einsum('gnp,btgp->btgn', bbar_re, uf)
    xi = jnp.einsum('gnp,btgp->btgn', bbar_im, uf)
    h0r, h0i = h0_re.astype(f32), h0_im.astype(f32)
    xr = xr.at[:, 0].add(ar * h0r - ai * h0i)
    xi = xi.at[:, 0].add(ar * h0i + ai * h0r)
    a_r = jnp.broadcast_to(ar, xr.shape)
    a_i = jnp.broadcast_to(ai, xi.shape)

    def combine(e1, e2):
        a1r, a1i, b1r, b1i = e1
        a2r, a2i, b2r, b2i = e2
        return (a2r * a1r - a2i * a1i, a2r * a1i + a2i * a1r,
                a2r * b1r - a2i * b1i + b2r, a2r * b1i + a2i * b1r + b2i)

    _, _, hr, hi = lax.associative_scan(combine, (a_r, a_i, xr, xi), axis=1)
    y = (jnp.einsum('gpn,btgn->btgp', c_re.astype(f32), hr)
         - jnp.einsum('gpn,btgn->btgp', c_im.astype(f32), hi)
         + d_skip.astype(f32).reshape(SSM_G, SSM_P) * uf)
    return y.reshape(bsz, t, W_SSM).astype(u.dtype), hr[:, -1], hi[:, -1]


def _layer(x, past_kd, past_vd, past_ks, past_vs, h0_re, h0_im, lam_init, p):
    (norm_g, w_in, lam_re, lam_im, log_dt, b_re, b_im, c_re, c_im, d_skip, w_glu,
     qn_g, kn_g, lq1, lk1, lq2, lk2, subln_g, w_out) = p
    f32 = jnp.float32
    bsz, t = x.shape[0], x.shape[1]
    widths = [W_SSM, W_SSM, W_DIFF, W_DIFF, W_DIFF, W_DIFF, W_SB, W_SB, W_SB, W_SB]
    cuts = [int(c) for c in np.cumsum(widths)[:-1]]
    u, g_a, q_d, k_d, v_d, g_d, q_s, k_s, v_s, g_s = jnp.split(_rms(x, norm_g) @ w_in, cuts, axis=-1)

    y_a, hT_re, hT_im = _s5(u, h0_re, h0_im, lam_re, lam_im, log_dt, b_re, b_im, c_re, c_im, d_skip)
    y_a = jax.nn.gelu(y_a)
    o_a = y_a * jax.nn.sigmoid(y_a @ w_glu) * jax.nn.silu(g_a)

    q_d = _rms(q_d.reshape(bsz, t, DIFF_H, 2, DIFF_HALF), qn_g)
    k_d = _rms(k_d.reshape(bsz, t, DIFF_H, 2, DIFF_HALF), kn_g)
    v_d = v_d.reshape(bsz, t, DIFF_H, DIFF_HD)
    n_past = past_kd.shape[1]
    k_all = jnp.concatenate([past_kd.reshape(bsz, n_past, DIFF_H, 2, DIFF_HALF), k_d], axis=1)
    v_all = jnp.concatenate([past_vd, v_d], axis=1)
    lam = (jnp.exp(jnp.sum(lq1.astype(f32) * lk1.astype(f32)))
           - jnp.exp(jnp.sum(lq2.astype(f32) * lk2.astype(f32))) + lam_init)
    o_d = _diff_attention(q_d, k_all, v_all, lam)
    o_d = (_rms(o_d, subln_g) * (1.0 - lam_init)).reshape(bsz, t, W_DIFF) * jax.nn.silu(g_d)

    q_s = q_s.reshape(bsz, t, SB_H, SB_HD)
    k_s = k_s.reshape(bsz, t, SB_H, SB_HD)
    v_s = v_s.reshape(bsz, t, SB_H, SB_HD)
    o_s = _stick_breaking(q_s, jnp.concatenate([past_ks, k_s], axis=1),
                          jnp.concatenate([past_vs, v_s], axis=1))
    o_s = o_s.reshape(bsz, t, W_SB) * jax.nn.silu(g_s)

    y = jnp.concatenate([o_a, o_d, o_s], axis=-1) @ w_out
    new = (k_d.reshape(bsz, t, DIFF_H, DIFF_HD), v_d, k_s, v_s, hT_re, hT_im)
    return x + y.astype(x.dtype), new


def setup_inputs(seed: int = 0) -> dict:
    key = jax.random.key(seed)
    ks = jax.random.split(key, 32)
    f32 = jnp.float32
    n_pages = PAST_LEN // PAGE_SIZE
    n_phys = (5 * DEC_BATCH * n_pages) // 4

    def nrm(k, shape, s):
        return s * jax.random.normal(k, shape, f32)

    x_prompt = nrm(ks[0], (BATCH, SEQ, D_MODEL), 1.0)
    x_sample = nrm(ks[1], (DEC_BATCH, DEC_SEQ, D_MODEL), 1.0)
    cache_k_diff = nrm(ks[2], (DEPTH, n_phys, PAGE_SIZE, DIFF_H, DIFF_HD), 1.0)
    cache_v_diff = nrm(ks[3], (DEPTH, n_phys, PAGE_SIZE, DIFF_H, DIFF_HD), 1.0)
    cache_k_sb = nrm(ks[4], (DEPTH, n_phys, PAGE_SIZE, SB_H, SB_HD), 1.0)
    cache_v_sb = nrm(ks[5], (DEPTH, n_phys, PAGE_SIZE, SB_H, SB_HD), 1.0)
    state_ssm_re = nrm(ks[6], (DEPTH, DEC_BATCH, SSM_G, SSM_N), 0.1)
    state_ssm_im = nrm(ks[7], (DEPTH, DEC_BATCH, SSM_G, SSM_N), 0.1)
    page_table = jax.random.permutation(ks[8], n_phys)[: DEC_BATCH * n_pages].reshape(
        DEC_BATCH, n_pages).astype(jnp.int32)
    norm_g = 1.0 + nrm(ks[9], (DEPTH, D_MODEL), 0.01)
    w_in = nrm(ks[10], (DEPTH, D_MODEL, IN_COLS), D_MODEL ** -0.5)
    ssm_lam_re = -0.5 + nrm(ks[11], (DEPTH, SSM_G, SSM_N), 0.01)
    ssm_lam_im = math.pi * jnp.arange(SSM_N, dtype=f32) + nrm(ks[12], (DEPTH, SSM_G, SSM_N), 0.01)
    ssm_log_dt = jax.random.uniform(ks[13], (DEPTH, SSM_G), f32, math.log(1e-3), math.log(1e-1))
    ssm_b_re = nrm(ks[14], (DEPTH, SSM_G, SSM_N, SSM_P), (2 * SSM_P) ** -0.5)
    ssm_b_im = nrm(ks[15], (DEPTH, SSM_G, SSM_N, SSM_P), (2 * SSM_P) ** -0.5)
    ssm_c_re = nrm(ks[16], (DEPTH, SSM_G, SSM_P, SSM_N), SSM_N ** -0.5)
    ssm_c_im = nrm(ks[17], (DEPTH, SSM_G, SSM_P, SSM_N), SSM_N ** -0.5)
    ssm_d = nrm(ks[18], (DEPTH, W_SSM), 1.0)
    w_glu = nrm(ks[19], (DEPTH, W_SSM, W_SSM), W_SSM ** -0.5)
    diff_qn_g = 1.0 + nrm(ks[20], (DEPTH, DIFF_HALF), 0.01)
    diff_kn_g = 1.0 + nrm(ks[21], (DEPTH, DIFF_HALF), 0.01)
    diff_lq1 = nrm(ks[22], (DEPTH, DIFF_HALF), 0.1)
    diff_lk1 = nrm(ks[23], (DEPTH, DIFF_HALF), 0.1)
    diff_lq2 = nrm(ks[24], (DEPTH, DIFF_HALF), 0.1)
    diff_lk2 = nrm(ks[25], (DEPTH, DIFF_HALF), 0.1)
    diff_subln_g = 1.0 + nrm(ks[26], (DEPTH, DIFF_HD), 0.01)
    w_out = nrm(ks[27], (DEPTH, W_MIX, D_MODEL), W_MIX ** -0.5)
    return {'x_prompt': x_prompt, 'x_sample': x_sample,
            'cache_k_diff': cache_k_diff, 'cache_v_diff': cache_v_diff,
            'cache_k_sb': cache_k_sb, 'cache_v_sb': cache_v_sb,
            'state_ssm_re': state_ssm_re, 'state_ssm_im': state_ssm_im,
            'page_table': page_table,
            'norm_g': norm_g, 'w_in': w_in,
            'ssm_lam_re': ssm_lam_re, 'ssm_lam_im': ssm_lam_im, 'ssm_log_dt': ssm_log_dt,
            'ssm_b_re': ssm_b_re, 'ssm_b_im': ssm_b_im, 'ssm_c_re': ssm_c_re, 'ssm_c_im': ssm_c_im,
            'ssm_d': ssm_d, 'w_glu': w_glu,
            'diff_qn_g': diff_qn_g, 'diff_kn_g': diff_kn_g,
            'diff_lq1': diff_lq1, 'diff_lk1': diff_lk1, 'diff_lq2': diff_lq2, 'diff_lk2': diff_lk2,
            'diff_subln_g': diff_subln_g, 'w_out': w_out}


def reference(x_prompt, x_sample, cache_k_diff, cache_v_diff, cache_k_sb, cache_v_sb,
              state_ssm_re, state_ssm_im, page_table, norm_g, w_in,
              ssm_lam_re, ssm_lam_im, ssm_log_dt, ssm_b_re, ssm_b_im, ssm_c_re, ssm_c_im,
              ssm_d, w_glu, diff_qn_g, diff_kn_g, diff_lq1, diff_lk1, diff_lq2, diff_lk2,
              diff_subln_g, w_out):
    n_pages = page_table.shape[1]
    past_len = n_pages * cache_k_diff.shape[2]

    def gather(pool):
        g = pool[page_table]
        return g.reshape((g.shape[0], past_len) + g.shape[3:])

    bp = x_prompt.shape[0]
    empty_d = jnp.zeros((bp, 0, DIFF_H, DIFF_HD), x_prompt.dtype)
    empty_s = jnp.zeros((bp, 0, SB_H, SB_HD), x_prompt.dtype)
    h0_zero = jnp.zeros((bp, SSM_G, SSM_N), jnp.float32)

    xp, xs = x_prompt, x_sample
    rows_p, rows_s = [], []
    for l in range(DEPTH):
        lam_init = 0.8 - 0.6 * math.exp(-0.3 * l)
        p = (norm_g[l], w_in[l], ssm_lam_re[l], ssm_lam_im[l], ssm_log_dt[l], ssm_b_re[l], ssm_b_im[l],
             ssm_c_re[l], ssm_c_im[l], ssm_d[l], w_glu[l], diff_qn_g[l], diff_kn_g[l],
             diff_lq1[l], diff_lk1[l], diff_lq2[l], diff_lk2[l], diff_subln_g[l], w_out[l])
        xp, new_p = _layer(xp, empty_d, empty_d, empty_s, empty_s, h0_zero, h0_zero, lam_init, p)
        xs, new_s = _layer(xs, gather(cache_k_diff[l]), gather(cache_v_diff[l]),
                           gather(cache_k_sb[l]), gather(cache_v_sb[l]),
                           state_ssm_re[l], state_ssm_im[l], lam_init, p)
        rows_p.append(new_p)
        rows_s.append(new_s)

    (new_k_diff_prompt, new_v_diff_prompt, new_k_sb_prompt, new_v_sb_prompt,
     new_ssm_re_prompt, new_ssm_im_prompt) = [jnp.stack(r) for r in zip(*rows_p)]
    (new_k_diff_sample, new_v_diff_sample, new_k_sb_sample, new_v_sb_sample,
     new_ssm_re_sample, new_ssm_im_sample) = [jnp.stack(r) for r in zip(*rows_s)]
    return (xp, xs,
            new_k_diff_prompt, new_v_diff_prompt, new_k_sb_prompt, new_v_sb_prompt,
            new_ssm_re_prompt, new_ssm_im_prompt,
            new_k_diff_sample, new_v_diff_sample, new_k_sb_sample, new_v_sb_sample,
            new_ssm_re_sample, new_ssm_im_sample)
```

```python
import functools
import math

import jax
import jax.numpy as jnp
from jax import lax
from jax.experimental import pallas as pl
from jax.experimental.pallas import tpu as pltpu

F32 = jnp.float32
BF16 = jnp.bfloat16

EPS = 1e-6
LANES = 128
SUBLANES = 8
HEAD_DIM = 128
DIFF_HALF = HEAD_DIM // 2
SSM_P = 16
SSM_N = 64
SSM_CHUNK = LANES // SSM_P
VMEM_LIMIT = 56 * 1024 * 1024


def _params(sem):
    return pltpu.CompilerParams(dimension_semantics=sem, vmem_limit_bytes=VMEM_LIMIT)


def _split_hi_lo(t):
    hi = t.astype(BF16)
    lo = (t - hi.astype(F32)).astype(BF16)
    return hi, lo


def _dot(a, b):
    return jnp.dot(a, b, preferred_element_type=F32)


def _dot_nt(a, b):
    return lax.dot_general(a, b, (((1,), (1,)), ((), ())), preferred_element_type=F32)


def _silu(x):
    return x * (1.0 / (1.0 + jnp.exp(-x)))


def _half_rms(y, pm, gain):
    hi, lo = _split_hi_lo(y * y)
    ms = _dot(hi, pm) + _dot(lo, pm)
    return y * lax.rsqrt(ms + EPS) * gain


def _normed_rows(x_ref, g_ref, xn_ref):
    @pl.when(pl.program_id(1) == 0)
    def _():
        x = x_ref[...]
        ms = jnp.mean(x * x, axis=-1, keepdims=True)
        xn_ref[...] = (x * lax.rsqrt(ms + EPS) * g_ref[...]).astype(BF16)


def _proj_kernel(x_ref, g_ref, w_ref, o_ref, xn_ref):
    _normed_rows(x_ref, g_ref, xn_ref)
    o_ref[...] = _dot(xn_ref[...], w_ref[...])


def _proj(x2, norm_g, w_bf, *, tm, tn):
    rows, d = x2.shape
    ncol = w_bf.shape[1]
    return pl.pallas_call(
        _proj_kernel,
        grid=(rows // tm, ncol // tn),
        in_specs=[
            pl.BlockSpec((tm, d), lambda i, j: (i, 0)),
            pl.BlockSpec((1, d), lambda i, j: (0, 0)),
            pl.BlockSpec((d, tn), lambda i, j: (0, j)),
        ],
        out_specs=pl.BlockSpec((tm, tn), lambda i, j: (i, j)),
        out_shape=jax.ShapeDtypeStruct((rows, ncol), F32),
        scratch_shapes=[pltpu.VMEM((tm, d), BF16)],
        compiler_params=_params(("parallel", "arbitrary")),
        name="proj",
    )(x2, norm_g, w_bf)


def _kv_heads(y, kg_ref, pm_ref, normed):
    heads = [y[:, h * HEAD_DIM:(h + 1) * HEAD_DIM] for h in range(y.shape[1] // HEAD_DIM)]
    if normed:
        heads = [_half_rms(yh, pm_ref[...], kg_ref[...]) for yh in heads]
    return heads


def _kvproj_prompt_kernel(x_ref, g_ref, w_ref, kg_ref, pm_ref, a0, a1, a2, a3,
                          kd_ref, vd_ref, ks_ref, vs_ref, xn_ref):
    del a0, a1, a2, a3
    _normed_rows(x_ref, g_ref, xn_ref)
    j = pl.program_id(1)
    y = _dot(xn_ref[...], w_ref[...])
    for idx, ref in enumerate((kd_ref, vd_ref, ks_ref, vs_ref)):
        @pl.when(j == idx)
        def _(ref=ref, idx=idx):
            for h, yh in enumerate(_kv_heads(y, kg_ref, pm_ref, idx == 0)):
                ref[h] = yh


def _kvproj_prompt(x2, norm_g, w_bf, k_gain, pmat, stacks, *, layer, seq, tm):
    rows, d = x2.shape
    width = w_bf.shape[1] // 4
    n_heads = width // HEAD_DIM
    tpb = seq // tm
    any_spec = pl.BlockSpec(memory_space=pl.ANY)
    out_spec = pl.BlockSpec((None, None, n_heads, tm, HEAD_DIM),
                            lambda i, j: (layer, i // tpb, 0, i % tpb, 0))
    return pl.pallas_call(
        _kvproj_prompt_kernel,
        grid=(rows // tm, 4),
        in_specs=[
            pl.BlockSpec((tm, d), lambda i, j: (i, 0)),
            pl.BlockSpec((1, d), lambda i, j: (0, 0)),
            pl.BlockSpec((d, width), lambda i, j: (0, j)),
            pl.BlockSpec((1, HEAD_DIM), lambda i, j: (0, 0)),
            pl.BlockSpec((LANES, LANES), lambda i, j: (0, 0)),
            any_spec, any_spec, any_spec, any_spec,
        ],
        out_specs=[out_spec] * 4,
        out_shape=[jax.ShapeDtypeStruct(s.shape, s.dtype) for s in stacks],
        input_output_aliases={5: 0, 6: 1, 7: 2, 8: 3},
        scratch_shapes=[pltpu.VMEM((tm, d), BF16)],
        compiler_params=_params(("parallel", "arbitrary")),
        name="kvproj_prompt",
    )(x2, norm_g, w_bf, k_gain, pmat, *stacks)


def _kvproj_step_kernel(x_ref, g_ref, w_ref, kg_ref, pm_ref, o_ref, xn_ref):
    _normed_rows(x_ref, g_ref, xn_ref)
    j = pl.program_id(1)
    y = _dot(xn_ref[...], w_ref[...])
    for normed in (True, False):
        @pl.when((j == 0) == normed)
        def _(normed=normed):
            for h, yh in enumerate(_kv_heads(y, kg_ref, pm_ref, normed)):
                o_ref[:, h * HEAD_DIM:(h + 1) * HEAD_DIM] = yh


def _kvproj_step(x2, norm_g, w_bf, k_gain, pmat):
    rows, d = x2.shape
    ncol = w_bf.shape[1]
    width = ncol // 4
    return pl.pallas_call(
        _kvproj_step_kernel,
        grid=(1, 4),
        in_specs=[
            pl.BlockSpec((rows, d), lambda i, j: (i, 0)),
            pl.BlockSpec((1, d), lambda i, j: (0, 0)),
            pl.BlockSpec((d, width), lambda i, j: (0, j)),
            pl.BlockSpec((1, HEAD_DIM), lambda i, j: (0, 0)),
            pl.BlockSpec((LANES, LANES), lambda i, j: (0, 0)),
        ],
        out_specs=pl.BlockSpec((rows, width), lambda i, j: (i, j)),
        out_shape=jax.ShapeDtypeStruct((rows, ncol), F32),
        scratch_shapes=[pltpu.VMEM((rows, d), BF16)],
        compiler_params=_params(("parallel", "arbitrary")),
        name="kvproj_step",
    )(x2, norm_g, w_bf, k_gain, pmat)


def _s5_prep_kernel(lr_ref, li_ref, ldt_ref, br_ref, bi_ref, ar_ref, ai_ref, bbr_ref, bbi_ref):
    lr, li = lr_ref[...], li_ref[...]
    dt = jnp.exp(ldt_ref[...])
    mag = jnp.exp(lr * dt)
    ar, ai = mag * jnp.cos(li * dt), mag * jnp.sin(li * dt)
    den = lr * lr + li * li
    kr = ((ar - 1.0) * lr + ai * li) / den
    ki = (ai * lr - (ar - 1.0) * li) / den
    br, bi = br_ref[...], bi_ref[...]
    ar_ref[...] = ar
    ai_ref[...] = ai
    bbr_ref[...] = kr * br - ki * bi
    bbi_ref[...] = kr * bi + ki * br


def _s5_prep(lam_re, lam_im, log_dt, b_re, b_im):
    dep, g, n = lam_re.shape
    p = b_re.shape[-1]
    rows = dep * g * p

    def rep(a):
        return jnp.broadcast_to(a[:, :, None, :], (dep, g, p, n)).reshape(rows, n)

    ldt = jnp.broadcast_to(log_dt[:, :, None, None], (dep, g, p, n)).reshape(rows, n)
    brt = jnp.swapaxes(b_re, 2, 3).reshape(rows, n)
    bit = jnp.swapaxes(b_im, 2, 3).reshape(rows, n)
    shp = jax.ShapeDtypeStruct((rows, n), F32)
    ar, ai, bbr, bbi = pl.pallas_call(
        _s5_prep_kernel, out_shape=(shp, shp, shp, shp), name="s5_prep",
    )(rep(lam_re), rep(lam_im), ldt, brt, bit)
    ar = ar.reshape(dep, g, p, n)[:, :, 0]
    ai = ai.reshape(dep, g, p, n)[:, :, 0]
    return ar, ai, bbr.reshape(dep, g, p, n), bbi.reshape(dep, g, p, n)


def _s5_layouts(ar, ai, bbr, bbi, c_re, c_im):
    g, n = ar.shape
    nch = g // SSM_CHUNK
    half = SSM_CHUNK * n
    eye = jnp.eye(SSM_CHUNK, dtype=F32)

    def in_bd(bb):
        bb = bb.reshape(nch, SSM_CHUNK, SSM_P, n)
        return (bb[:, :, :, None, :] * eye[None, :, None, :, None]).reshape(nch, LANES, half)

    def out_bd(c):
        c = jnp.swapaxes(c.reshape(nch, SSM_CHUNK, SSM_P, n), 2, 3)
        return (c[:, :, :, None, :] * eye[None, :, None, :, None]).reshape(nch, half, LANES)

    bbd = jnp.concatenate([in_bd(bbr), in_bd(bbi)], axis=-1).astype(BF16)
    cbd = jnp.concatenate([out_bd(c_re), -out_bd(c_im)], axis=1).astype(BF16)
    return bbd, cbd, ar.reshape(nch, 1, half), ai.reshape(nch, 1, half)


def _cmul(ar, ai, br, bi):
    return ar * br - ai * bi, ar * bi + ai * br


def _s5_prompt_kernel(u_ref, bbd_ref, cbd_ref, ar_ref, ai_ref, d_ref, y_ref, ht_ref, x_scr,
                      *, nseg, seglen, unroll):
    half = ar_ref.shape[-1]
    jc = min(seglen, 64)
    rc = jc * nseg

    def fill(c, carry):
        r0 = pl.multiple_of(c * rc, rc)
        x = _dot(u_ref[pl.ds(r0, rc), :].astype(BF16), bbd_ref[...])
        x_scr[pl.ds(c * jc, jc)] = x.reshape(jc, nseg, 2 * half)
        return carry

    lax.fori_loop(0, seglen // jc, fill, 0)
    ar = jnp.broadcast_to(ar_ref[...], (nseg, half))
    ai = jnp.broadcast_to(ai_ref[...], (nseg, half))

    def scan_step(j, carry):
        hr, hi = carry
        pr, pi = _cmul(ar, ai, hr, hi)
        hr = pr + x_scr[j, :, :half]
        hi = pi + x_scr[j, :, half:]
        x_scr[j, :, :half] = hr
        x_scr[j, :, half:] = hi
        return hr, hi

    zero = jnp.zeros((nseg, half), F32)
    er, ei = lax.fori_loop(0, seglen, scan_step, (zero, zero), unroll=unroll)

    sr, si = ar_ref[...], ai_ref[...]
    for _ in range(int(math.log2(seglen))):
        sr, si = _cmul(sr, si, sr, si)

    cin_r, cin_i = [jnp.zeros((1, half), F32)], [jnp.zeros((1, half), F32)]
    for s in range(nseg):
        pr, pi = _cmul(sr, si, cin_r[-1], cin_i[-1])
        cin_r.append(er[s:s + 1] + pr)
        cin_i.append(ei[s:s + 1] + pi)
    ht_ref[:, :half] = cin_r[-1]
    ht_ref[:, half:] = cin_i[-1]
    cr = jnp.concatenate(cin_r[:-1], axis=0)
    ci = jnp.concatenate(cin_i[:-1], axis=0)

    def fix_step(j, carry):
        pr, pi = carry
        dr, di = _cmul(pr, pi, cr, ci)
        x_scr[j, :, :half] = x_scr[j, :, :half] + dr
        x_scr[j, :, half:] = x_scr[j, :, half:] + di
        return _cmul(pr, pi, ar, ai)

    lax.fori_loop(0, seglen, fix_step, (ar, ai), unroll=unroll)

    def drain(c, carry):
        r0 = pl.multiple_of(c * rc, rc)
        h = x_scr[pl.ds(c * jc, jc)].reshape(rc, 2 * half)
        y = _dot(h.astype(BF16), cbd_ref[...]) + d_ref[...] * u_ref[pl.ds(r0, rc), :]
        y_ref[pl.ds(r0, rc), :] = jax.nn.gelu(y)
        return carry

    lax.fori_loop(0, seglen // jc, drain, 0)


def _s5_prompt(u_perm, bbd, cbd, ar_l, ai_l, d_skip, *, nseg):
    b, t, w = u_perm.shape
    nch, _, cols = bbd.shape
    half = cols // 2
    seglen = t // nseg
    kern = functools.partial(_s5_prompt_kernel, nseg=nseg, seglen=seglen, unroll=min(8, seglen))
    return pl.pallas_call(
        kern,
        grid=(b, nch),
        in_specs=[
            pl.BlockSpec((None, t, LANES), lambda i, c: (i, 0, c)),
            pl.BlockSpec((None, LANES, cols), lambda i, c: (c, 0, 0)),
            pl.BlockSpec((None, cols, LANES), lambda i, c: (c, 0, 0)),
            pl.BlockSpec((None, 1, half), lambda i, c: (c, 0, 0)),
            pl.BlockSpec((None, 1, half), lambda i, c: (c, 0, 0)),
            pl.BlockSpec((1, LANES), lambda i, c: (0, c)),
        ],
        out_specs=[
            pl.BlockSpec((None, t, LANES), lambda i, c: (i, 0, c)),
            pl.BlockSpec((None, None, 1, cols), lambda i, c: (i, c, 0, 0)),
        ],
        out_shape=[jax.ShapeDtypeStruct((b, t, w), F32),
                   jax.ShapeDtypeStruct((b, nch, 1, cols), F32)],
        scratch_shapes=[pltpu.VMEM((seglen, nseg, cols), F32)],
        compiler_params=_params(("parallel", "arbitrary")),
        name="s5_prompt",
    )(u_perm, bbd, cbd, ar_l, ai_l, d_skip)


def _s5_step_kernel(p_ref, h0_ref, bbd_ref, cbd_ref, ar_ref, ai_ref, d_ref, y_ref, ht_ref, *, nch):
    half = ar_ref.shape[-1]
    for c in range(nch):
        u = p_ref[:, c * LANES:(c + 1) * LANES]
        x = _dot(u.astype(BF16), bbd_ref[c])
        h0r, h0i = h0_ref[c, :, :half], h0_ref[c, :, half:]
        pr, pi = _cmul(ar_ref[c], ai_ref[c], h0r, h0i)
        hr = pr + x[:, :half]
        hi = pi + x[:, half:]
        ht_ref[c, :, :half] = hr
        ht_ref[c, :, half:] = hi
        h = jnp.concatenate([hr, hi], axis=-1)
        y = _dot(h.astype(BF16), cbd_ref[c]) + d_ref[:, c * LANES:(c + 1) * LANES] * u
        y_ref[:, c * LANES:(c + 1) * LANES] = jax.nn.gelu(y)


def _s5_step(proj_s, h0_l, bbd, cbd, ar_l, ai_l, d_skip):
    b = proj_s.shape[0]
    nch, _, cols = bbd.shape
    w = nch * LANES
    kern = functools.partial(_s5_step_kernel, nch=nch)
    return pl.pallas_call(
        kern,
        out_shape=[jax.ShapeDtypeStruct((b, w), F32), jax.ShapeDtypeStruct((nch, b, cols), F32)],
        compiler_params=pltpu.CompilerParams(vmem_limit_bytes=VMEM_LIMIT),
        name="s5_step",
    )(proj_s, h0_l, bbd, cbd, ar_l, ai_l, d_skip)


def _diff_lambda(lq1_ref, lk1_ref, lq2_ref, lk2_ref, lam_init):
    s1 = jnp.sum(lq1_ref[...] * lk1_ref[...], axis=-1, keepdims=True)
    s2 = jnp.sum(lq2_ref[...] * lk2_ref[...], axis=-1, keepdims=True)
    return jnp.exp(s1) - jnp.exp(s2) + lam_init


def _diff_prompt_kernel(q_ref, k_ref, v_ref, g_ref, lq1_ref, lk1_ref, lq2_ref, lk2_ref, sg_ref,
                        qg_ref, pm_ref, o_ref, *, blk, lam_init):
    qi = pl.program_id(2)
    q = _half_rms(q_ref[...], pm_ref[...], qg_ref[...]) * (DIFF_HALF ** -0.5)
    lane = lax.broadcasted_iota(jnp.int32, q.shape, 1)
    q2 = jnp.concatenate([jnp.where(lane < DIFF_HALF, q, 0.0),
                          jnp.where(lane >= DIFF_HALF, q, 0.0)], axis=0).astype(BF16)

    def step(ki, carry, masked):
        m, l, acc = carry
        start = pl.multiple_of(ki * blk, blk)
        k = k_ref[pl.ds(start, blk), :].astype(BF16)
        v = v_ref[pl.ds(start, blk), :].astype(BF16)
        s = _dot_nt(q2, k)
        if masked:
            row = lax.broadcasted_iota(jnp.int32, s.shape, 0)
            col = lax.broadcasted_iota(jnp.int32, s.shape, 1)
            row = jnp.where(row >= blk, row - blk, row)
            s = jnp.where(col <= row, s, -jnp.inf)
        m_new = jnp.maximum(m, jnp.max(s, axis=-1, keepdims=True))
        alpha = jnp.exp(m - m_new)
        p = jnp.exp(s - m_new)
        l = alpha * l + jnp.sum(p, axis=-1, keepdims=True)
        acc = alpha * acc + _dot(p.astype(BF16), v)
        return m_new, l, acc

    init = (jnp.full((2 * blk, 1), -jnp.inf, F32), jnp.zeros((2 * blk, 1), F32),
            jnp.zeros((2 * blk, HEAD_DIM), F32))
    carry = step(qi, init, True)
    m, l, acc = lax.fori_loop(0, qi, lambda ki, c: step(ki, c, False), carry)

    lam = _diff_lambda(lq1_ref, lk1_ref, lq2_ref, lk2_ref, lam_init)
    o = acc[:blk] / l[:blk] - lam * (acc[blk:] / l[blk:])
    ms = jnp.mean(o * o, axis=-1, keepdims=True)
    o = o * lax.rsqrt(ms + EPS) * sg_ref[...] * (1.0 - lam_init)
    o_ref[...] = (o * _silu(g_ref[...])).astype(BF16)


def _sb_scores(z):
    sp = jnp.log1p(jnp.exp(-jnp.abs(z)))
    return jnp.minimum(z, 0.0) - sp, -jnp.maximum(z, 0.0) - sp


def _strict_lower(n):
    r = lax.broadcasted_iota(jnp.int32, (n, n), 0)
    c = lax.broadcasted_iota(jnp.int32, (n, n), 1)
    return jnp.where(r > c, 1.0, 0.0).astype(BF16)


def _sum_later(lk, tri):
    hi, lo = _split_hi_lo(lk)
    return _dot(hi, tri) + _dot(lo, tri)


def _sb_prompt_kernel(q_ref, k_ref, v_ref, g_ref, o_ref, *, blk):
    qi = pl.program_id(2)
    q = (q_ref[...] * (HEAD_DIM ** -0.5)).astype(BF16)
    tri = _strict_lower(blk)

    def step(ki, carry, masked):
        r, acc = carry
        start = pl.multiple_of(ki * blk, blk)
        k = k_ref[pl.ds(start, blk), :].astype(BF16)
        v = v_ref[pl.ds(start, blk), :].astype(BF16)
        lb, lk = _sb_scores(_dot_nt(q, k))
        if masked:
            row = lax.broadcasted_iota(jnp.int32, lb.shape, 0)
            col = lax.broadcasted_iota(jnp.int32, lb.shape, 1)
            mask = col < row
            lk = jnp.where(mask, lk, 0.0)
        a = jnp.exp(lb + (r + _sum_later(lk, tri)))
        if masked:
            a = jnp.where(mask, a, 0.0)
        acc = acc + _dot(a.astype(BF16), v)
        return r + jnp.sum(lk, axis=-1, keepdims=True), acc

    carry = step(qi, (jnp.zeros((blk, 1), F32), jnp.zeros((blk, HEAD_DIM), F32)), True)
    _, acc = lax.fori_loop(0, qi, lambda kk, c: step(qi - 1 - kk, c, False), carry)
    o_ref[...] = (acc * _silu(g_ref[...])).astype(BF16)


def _prompt_attention(kern, proj3, k_stack, v_stack, extra, *, layer, blk, col_q, col_g, name):
    b, t, _ = proj3.shape
    n_heads = k_stack.shape[2]
    qspec = lambda c0: pl.BlockSpec((None, blk, HEAD_DIM), lambda i, h, q: (i, q, c0 + h))
    kvspec = pl.BlockSpec((None, None, None, t, HEAD_DIM), lambda i, h, q: (layer, i, h, 0, 0))
    small = [pl.BlockSpec(e.shape, lambda i, h, q: (0, 0)) for e in extra]
    return pl.pallas_call(
        kern,
        grid=(b, n_heads, t // blk),
        in_specs=[qspec(col_q), kvspec, kvspec, qspec(col_g)] + small,
        out_specs=pl.BlockSpec((None, blk, HEAD_DIM), lambda i, h, q: (i, q, h)),
        out_shape=jax.ShapeDtypeStruct((b, t, n_heads * HEAD_DIM), BF16),
        compiler_params=_params(("parallel", "parallel", "arbitrary")),
        name=name,
    )(proj3, k_stack, v_stack, proj3, *extra)


def _diff_decode_kernel(pt_ref, p_ref, kv_ref, lq1_ref, lk1_ref, lq2_ref, lk2_ref, sg_ref, qg_ref,
                        pm_ref, *rest, n_pp, col_q, col_g, n_heads, lam_init):
    del pt_ref
    k_refs, v_refs = rest[:n_pp], rest[n_pp:2 * n_pp]
    o_ref, wq_scr, m_scr, l_scr, acc_scr = rest[2 * n_pp:]
    width = n_heads * HEAD_DIM
    s_idx = pl.program_id(1)

    @pl.when(s_idx == 0)
    def _():
        row = lax.broadcasted_iota(jnp.int32, (SUBLANES, HEAD_DIM), 0)
        lane = lax.broadcasted_iota(jnp.int32, (SUBLANES, HEAD_DIM), 1)
        keep = jnp.logical_or(jnp.logical_and(row == 0, lane < DIFF_HALF),
                              jnp.logical_and(row == 1, lane >= DIFF_HALF))
        for h in range(n_heads):
            qh = p_ref[:, col_q + h * HEAD_DIM:col_q + (h + 1) * HEAD_DIM]
            qh = _half_rms(jnp.broadcast_to(qh, (SUBLANES, HEAD_DIM)), pm_ref[...], qg_ref[...])
            wq_scr[h] = jnp.where(keep, qh * (DIFF_HALF ** -0.5), 0.0).astype(BF16)
        m_scr[...] = jnp.full(m_scr.shape, -jnp.inf, F32)
        l_scr[...] = jnp.zeros(l_scr.shape, F32)
        acc_scr[...] = jnp.zeros(acc_scr.shape, F32)

    for h in range(n_heads):
        wq = wq_scr[h]
        s = jnp.concatenate([_dot_nt(wq, k_refs[j][h].astype(BF16)) for j in range(n_pp)], axis=-1)
        m_old = m_scr[h]
        m_new = jnp.maximum(m_old, jnp.max(s, axis=-1, keepdims=True))
        alpha = jnp.exp(m_old - m_new)
        p = jnp.exp(s - m_new)
        l_scr[h] = alpha * l_scr[h] + jnp.sum(p, axis=-1, keepdims=True)
        pv = _dot(p[:, :LANES].astype(BF16), v_refs[0][h].astype(BF16))
        for j in range(1, n_pp):
            pv = pv + _dot(p[:, j * LANES:(j + 1) * LANES].astype(BF16), v_refs[j][h].astype(BF16))
        acc_scr[h] = alpha * acc_scr[h] + pv
        m_scr[h] = m_new

    @pl.when(s_idx == pl.num_programs(1) - 1)
    def _():
        lam = _diff_lambda(lq1_ref, lk1_ref, lq2_ref, lk2_ref, lam_init)
        for h in range(n_heads):
            sl = slice(h * HEAD_DIM, (h + 1) * HEAD_DIM)
            k_new = kv_ref[:, h * HEAD_DIM:(h + 1) * HEAD_DIM].astype(BF16).astype(F32)
            v_new = kv_ref[:, width + h * HEAD_DIM:width + (h + 1) * HEAD_DIM].astype(BF16).astype(F32)
            s_new = jnp.sum(wq_scr[h].astype(F32) * k_new, axis=-1, keepdims=True)
            m_old = m_scr[h]
            m_fin = jnp.maximum(m_old, s_new)
            a_fin = jnp.exp(m_old - m_fin)
            p_new = jnp.exp(s_new - m_fin)
            l_fin = a_fin * l_scr[h] + p_new
            on = (a_fin * acc_scr[h] + p_new.astype(BF16).astype(F32) * v_new) / l_fin
            o = on[0:1] - lam * on[1:2]
            ms = jnp.mean(o * o, axis=-1, keepdims=True)
            o = o * lax.rsqrt(ms + EPS) * sg_ref[...] * (1.0 - lam_init)
            g = p_ref[:, col_g + h * HEAD_DIM:col_g + (h + 1) * HEAD_DIM]
            o_ref[:, sl] = (o * _silu(g)).astype(BF16)


def _sb_decode_kernel(pt_ref, p_ref, *rest, n_pp, col_q, col_g, n_heads):
    del pt_ref
    k_refs, v_refs = rest[:n_pp], rest[n_pp:2 * n_pp]
    o_ref, wq_scr, r_scr, acc_scr = rest[2 * n_pp:]
    s_idx = pl.program_id(1)

    @pl.when(s_idx == 0)
    def _():
        row = lax.broadcasted_iota(jnp.int32, (SUBLANES, HEAD_DIM), 0)
        for h in range(n_heads):
            qh = p_ref[:, col_q + h * HEAD_DIM:col_q + (h + 1) * HEAD_DIM] * (HEAD_DIM ** -0.5)
            qh = jnp.broadcast_to(qh, (SUBLANES, HEAD_DIM))
            wq_scr[h] = jnp.where(row == 0, qh, 0.0).astype(BF16)
        r_scr[...] = jnp.zeros(r_scr.shape, F32)
        acc_scr[...] = jnp.zeros(acc_scr.shape, F32)

    tri = _strict_lower(LANES)
    for h in range(n_heads):
        wq = wq_scr[h]
        r = r_scr[h]
        acc = acc_scr[h]
        for j in range(n_pp):
            lb, lk = _sb_scores(_dot_nt(wq, k_refs[j][h].astype(BF16)))
            a = jnp.exp(lb + (r + _sum_later(lk, tri)))
            acc = acc + _dot(a.astype(BF16), v_refs[j][h].astype(BF16))
            r = r + jnp.sum(lk, axis=-1, keepdims=True)
        r_scr[h] = r
        acc_scr[h] = acc

    @pl.when(s_idx == pl.num_programs(1) - 1)
    def _():
        for h in range(n_heads):
            g = p_ref[:, col_g + h * HEAD_DIM:col_g + (h + 1) * HEAD_DIM]
            o_ref[:, h * HEAD_DIM:(h + 1) * HEAD_DIM] = (acc_scr[h][0:1] * _silu(g)).astype(BF16)


def _decode_attention(kern, page_table, rows3, cache_k, cache_v, extra, scratch, *, layer,
                      n_pp, reverse, name):
    b = rows3[0].shape[0]
    n_pages = page_table.shape[1]
    _, _, n_heads, page, _ = cache_k.shape
    n_steps = n_pages // n_pp

    def page_spec(j):
        def imap(i, s, pt):
            slot = s * n_pp + j
            if reverse:
                slot = n_pages - 1 - slot
            return (layer, pt[i, slot], 0, 0, 0)
        return pl.BlockSpec((None, None, n_heads, page, HEAD_DIM), imap)

    per_seq = [pl.BlockSpec((None, 1, r.shape[2]), lambda i, s, pt: (i, 0, 0)) for r in rows3]
    small = [pl.BlockSpec(e.shape, lambda i, s, pt: (0, 0)) for e in extra]
    pages = [page_spec(j) for j in range(n_pp)]
    grid_spec = pltpu.PrefetchScalarGridSpec(
        num_scalar_prefetch=1,
        grid=(b, n_steps),
        in_specs=per_seq + small + pages + pages,
        out_specs=pl.BlockSpec((None, 1, n_heads * HEAD_DIM), lambda i, s, pt: (i, 0, 0)),
        scratch_shapes=scratch,
    )
    return pl.pallas_call(
        kern,
        grid_spec=grid_spec,
        out_shape=jax.ShapeDtypeStruct((b, 1, n_heads * HEAD_DIM), BF16),
        compiler_params=_params(("parallel", "arbitrary")),
        name=name,
    )(page_table, *rows3, *extra, *([cache_k] * n_pp), *([cache_v] * n_pp))


def _outproj_kernel(y_ref, ga_ref, od_ref, os_ref, x_ref, wg_ref, wa_ref, wd_ref, ws_ref, o_ref):
    y = y_ref[...]
    gate = 1.0 / (1.0 + jnp.exp(-_dot(y.astype(BF16), wg_ref[...])))
    oa = (y * gate * _silu(ga_ref[...])).astype(BF16)
    acc = _dot(oa, wa_ref[...]) + _dot(od_ref[...], wd_ref[...]) + _dot(os_ref[...], ws_ref[...])
    o_ref[...] = x_ref[...] + acc


def _outproj(y2, proj, o_d, o_s, x2, w_glu, w_a, w_d, w_s, *, tm, ga_block):
    rows, d = x2.shape
    wa = y2.shape[1]
    wd, ws = o_d.shape[1], o_s.shape[1]
    row = lambda w: pl.BlockSpec((tm, w), lambda i: (i, 0))
    full = lambda a: pl.BlockSpec(a.shape, lambda i: (0, 0))
    return pl.pallas_call(
        _outproj_kernel,
        grid=(rows // tm,),
        in_specs=[row(wa), pl.BlockSpec((tm, wa), lambda i: (i, ga_block)), row(wd), row(ws), row(d),
                  full(w_glu), full(w_a), full(w_d), full(w_s)],
        out_specs=row(d),
        out_shape=jax.ShapeDtypeStruct((rows, d), F32),
        compiler_params=_params(("parallel",)),
        name="outproj",
    )(y2, proj, o_d, o_s, x2, w_glu, w_a, w_d, w_s)


def _largest_tile(n, cap):
    t = min(n, cap)
    while n % t:
        t //= 2
    return t


def kernel(x_prompt, x_sample, cache_k_diff, cache_v_diff, cache_k_sb, cache_v_sb, state_ssm_re, state_ssm_im, page_table, norm_g, w_in, ssm_lam_re, ssm_lam_im, ssm_log_dt, ssm_b_re, ssm_b_im, ssm_c_re, ssm_c_im, ssm_d, w_glu, diff_qn_g, diff_kn_g, diff_lq1, diff_lk1, diff_lq2, diff_lk2, diff_subln_g, w_out):
    bp, t, d = x_prompt.shape
    bs = x_sample.shape[0]
    depth = w_in.shape[0]
    dh, sh = cache_k_diff.shape[3], cache_k_sb.shape[3]
    g, n = ssm_lam_re.shape[1], ssm_lam_re.shape[2]
    w_ssm = g * SSM_P
    w_diff, w_sb = dh * HEAD_DIM, sh * HEAD_DIM
    assert w_diff == w_sb
    nch = w_ssm // LANES
    half = SSM_CHUNK * n
    widths = [w_ssm, w_ssm] + [w_diff] * 4 + [w_sb] * 4
    offs = [0]
    for wd_ in widths:
        offs.append(offs[-1] + wd_)
    (r_u, r_ga, r_qd, r_kd, r_vd, r_gd, r_qs, r_ks, r_vs, r_gs, ncol) = offs
    assert ncol == w_in.shape[2]
    c_ga, c_qd, c_gd, c_qs, c_gs = w_ssm, 2 * w_ssm, 2 * w_ssm + w_diff, 2 * w_ssm + 2 * w_diff, \
        2 * w_ssm + 2 * w_diff + w_sb
    cols_a = c_gs + w_sb

    tn = 512
    assert cols_a % tn == 0
    tm_p = _largest_tile(t, 1024)
    tmo_p = _largest_tile(bp * t, 512)
    blk = _largest_tile(t, 256)
    nseg = SUBLANES
    n_pp = _largest_tile(page_table.shape[1], 8)

    lane_half = jnp.arange(LANES) // DIFF_HALF
    pmat = (lane_half[:, None] == lane_half[None, :]).astype(BF16) * (1.0 / DIFF_HALF)

    ar_all, ai_all, bbr_all, bbi_all = _s5_prep(ssm_lam_re, ssm_lam_im, ssm_log_dt, ssm_b_re, ssm_b_im)

    ck_d, cv_d, ck_s, cv_s = [jnp.transpose(c, (0, 1, 3, 2, 4))
                              for c in (cache_k_diff, cache_v_diff, cache_k_sb, cache_v_sb)]
    stacks = [jnp.zeros((depth, bp, dh, t, HEAD_DIM), F32), jnp.zeros((depth, bp, dh, t, HEAD_DIM), F32),
              jnp.zeros((depth, bp, sh, t, HEAD_DIM), F32), jnp.zeros((depth, bp, sh, t, HEAD_DIM), F32)]

    xp = x_prompt.reshape(bp * t, d)
    xs = x_sample.reshape(bs, d)
    ssm_p, rows_s = [], []
    for l in range(depth):
        lam_init = 0.8 - 0.6 * math.exp(-0.3 * l)
        wl = w_in[l]
        w_a_in = jnp.concatenate([wl[:, r_u:r_kd], wl[:, r_gd:r_ks], wl[:, r_gs:]], axis=1).astype(BF16)
        w_b_in = jnp.concatenate([wl[:, r_kd:r_gd], wl[:, r_ks:r_gs]], axis=1).astype(BF16)
        wo = w_out[l].astype(BF16)
        w_a, w_d, w_s = wo[:w_ssm], wo[w_ssm:w_ssm + w_diff], wo[w_ssm + w_diff:]
        wg = w_glu[l].astype(BF16)
        q_gain = jnp.tile(diff_qn_g[l], 2).reshape(1, HEAD_DIM)
        k_gain = jnp.tile(diff_kn_g[l], 2).reshape(1, HEAD_DIM)
        ng = norm_g[l].reshape(1, d)
        bbd, cbd, ar_l, ai_l = _s5_layouts(ar_all[l], ai_all[l], bbr_all[l], bbi_all[l],
                                           ssm_c_re[l], ssm_c_im[l])
        d_skip = ssm_d[l].reshape(1, w_ssm)
        lam_vecs = [v[l].reshape(1, DIFF_HALF) for v in (diff_lq1, diff_lk1, diff_lq2, diff_lk2)]
        diff_extra = lam_vecs + [diff_subln_g[l].reshape(1, HEAD_DIM), q_gain, pmat]

        proj = _proj(xp, ng, w_a_in, tm=tm_p, tn=tn)
        stacks = _kvproj_prompt(xp, ng, w_b_in, k_gain, pmat, stacks, layer=l, seq=t, tm=tm_p)
        proj3 = proj.reshape(bp, t, cols_a)
        u = proj3[:, :, :w_ssm].reshape(bp, nseg, t // nseg, w_ssm)
        u_perm = jnp.swapaxes(u, 1, 2).reshape(bp, t, w_ssm)
        y2p, ht = _s5_prompt(u_perm, bbd, cbd, ar_l, ai_l, d_skip, nseg=nseg)
        y2 = jnp.swapaxes(y2p.reshape(bp, t // nseg, nseg, w_ssm), 1, 2).reshape(bp * t, w_ssm)
        o_d = _prompt_attention(
            functools.partial(_diff_prompt_kernel, blk=blk, lam_init=lam_init), proj3,
            stacks[0], stacks[1], diff_extra, layer=l, blk=blk,
            col_q=c_qd // HEAD_DIM, col_g=c_gd // HEAD_DIM, name="diff_prompt")
        o_s = _prompt_attention(
            functools.partial(_sb_prompt_kernel, blk=blk), proj3, stacks[2], stacks[3], [],
            layer=l, blk=blk, col_q=c_qs // HEAD_DIM, col_g=c_gs // HEAD_DIM, name="sb_prompt")
        xp = _outproj(y2, proj, o_d.reshape(bp * t, w_diff), o_s.reshape(bp * t, w_sb), xp,
                      wg, w_a, w_d, w_s, tm=tmo_p, ga_block=c_ga // w_ssm)
        ht = ht.reshape(bp, nch, 2, SSM_CHUNK, n)
        ssm_p.append((ht[:, :, 0].reshape(bp, g, n), ht[:, :, 1].reshape(bp, g, n)))

        proj_s = _proj(xs, ng, w_a_in, tm=bs, tn=tn)
        kv_s = _kvproj_step(xs, ng, w_b_in, k_gain, pmat)
        proj_s3 = proj_s.reshape(bs, 1, cols_a)
        kv_s3 = kv_s.reshape(bs, 1, 4 * w_diff)
        h0 = jnp.concatenate([state_ssm_re[l].reshape(bs, nch, half),
                              state_ssm_im[l].reshape(bs, nch, half)], axis=-1)
        y2s, hts = _s5_step(proj_s, jnp.swapaxes(h0, 0, 1), bbd, cbd, ar_l, ai_l, d_skip)
        od_s = _decode_attention(
            functools.partial(_diff_decode_kernel, n_pp=n_pp, col_q=c_qd, col_g=c_gd, n_heads=dh,
                              lam_init=lam_init),
            page_table, [proj_s3, kv_s3[:, :, :2 * w_diff]], ck_d, cv_d, diff_extra,
            [pltpu.VMEM((dh, SUBLANES, HEAD_DIM), BF16), pltpu.VMEM((dh, SUBLANES, 1), F32),
             pltpu.VMEM((dh, SUBLANES, 1), F32), pltpu.VMEM((dh, SUBLANES, HEAD_DIM), F32)],
            layer=l, n_pp=n_pp, reverse=False, name="diff_decode")
        os_s = _decode_attention(
            functools.partial(_sb_decode_kernel, n_pp=n_pp, col_q=c_qs, col_g=c_gs, n_heads=sh),
            page_table, [proj_s3], ck_s, cv_s, [],
            [pltpu.VMEM((sh, SUBLANES, HEAD_DIM), BF16), pltpu.VMEM((sh, SUBLANES, 1), F32),
             pltpu.VMEM((sh, SUBLANES, HEAD_DIM), F32)],
            layer=l, n_pp=n_pp, reverse=True, name="sb_decode")
        xs = _outproj(y2s, proj_s, od_s.reshape(bs, w_diff), os_s.reshape(bs, w_sb), xs,
                      wg, w_a, w_d, w_s, tm=bs, ga_block=c_ga // w_ssm)
        hts = jnp.swapaxes(hts, 0, 1).reshape(bs, nch, 2, SSM_CHUNK, n)
        kv4 = kv_s.reshape(bs, 1, 4, dh, HEAD_DIM)
        rows_s.append((kv4[:, :, 0], kv4[:, :, 1], kv4[:, :, 2], kv4[:, :, 3],
                       hts[:, :, 0].reshape(bs, g, n), hts[:, :, 1].reshape(bs, g, n)))

    kv_p = [jnp.transpose(s, (0, 1, 3, 2, 4)) for s in stacks]
    ssm_p = [jnp.stack(r) for r in zip(*ssm_p)]
    outs_s = [jnp.stack(r) for r in zip(*rows_s)]
    return (xp.reshape(bp, t, d), xs.reshape(bs, 1, d), *kv_p, *ssm_p, *outs_s)
```

```python
import functools
import math

import jax
import jax.numpy as jnp
from jax import lax
from jax.experimental import pallas as pl
from jax.experimental.pallas import tpu as pltpu

F32 = jnp.float32
BF16 = jnp.bfloat16

EPS = 1e-6
LANES = 128
SUBLANES = 8
HEAD_DIM = 128
DIFF_HALF = HEAD_DIM // 2
SSM_P = 16
SSM_N = 64
SSM_CHUNK = LANES // SSM_P
SB_EXIT_LOG = -110.0
VMEM_LIMIT = 56 * 1024 * 1024


def _params(sem):
    return pltpu.CompilerParams(dimension_semantics=sem, vmem_limit_bytes=VMEM_LIMIT)


def _split_hi_lo(t):
    hi = t.astype(BF16)
    lo = (t - hi.astype(F32)).astype(BF16)
    return hi, lo


def _dot(a, b):
    return jnp.dot(a, b, preferred_element_type=F32)


def _dot_nt(a, b):
    return lax.dot_general(a, b, (((1,), (1,)), ((), ())), preferred_element_type=F32)


def _silu(x):
    return x * (1.0 / (1.0 + jnp.exp(-x)))


def _half_rms(y, pm, gain):
    hi, lo = _split_hi_lo(y * y)
    ms = _dot(hi, pm) + _dot(lo, pm)
    return y * lax.rsqrt(ms + EPS) * gain


def _normed_rows(x_ref, g_ref, xn_ref):
    @pl.when(pl.program_id(1) == 0)
    def _():
        x = x_ref[...]
        ms = jnp.mean(x * x, axis=-1, keepdims=True)
        xn_ref[...] = (x * lax.rsqrt(ms + EPS) * g_ref[...]).astype(BF16)


def _proj_kernel(x_ref, g_ref, w_ref, o_ref, xn_ref):
    _normed_rows(x_ref, g_ref, xn_ref)
    o_ref[...] = _dot(xn_ref[...], w_ref[...])


def _proj(x2, norm_g, w_bf, *, tm, tn):
    rows, d = x2.shape
    ncol = w_bf.shape[1]
    return pl.pallas_call(
        _proj_kernel,
        grid=(rows // tm, ncol // tn),
        in_specs=[
            pl.BlockSpec((tm, d), lambda i, j: (i, 0)),
            pl.BlockSpec((1, d), lambda i, j: (0, 0)),
            pl.BlockSpec((d, tn), lambda i, j: (0, j)),
        ],
        out_specs=pl.BlockSpec((tm, tn), lambda i, j: (i, j)),
        out_shape=jax.ShapeDtypeStruct((rows, ncol), F32),
        scratch_shapes=[pltpu.VMEM((tm, d), BF16)],
        compiler_params=_params(("parallel", "arbitrary")),
        name="proj",
    )(x2, norm_g, w_bf)


def _kv_heads(y, kg_ref, pm_ref, normed):
    heads = [y[:, h * HEAD_DIM:(h + 1) * HEAD_DIM] for h in range(y.shape[1] // HEAD_DIM)]
    if normed:
        heads = [_half_rms(yh, pm_ref[...], kg_ref[...]) for yh in heads]
    return heads


def _kvproj_prompt_kernel(x_ref, g_ref, w_ref, kg_ref, pm_ref, a0, a1, a2, a3,
                          kd_ref, vd_ref, ks_ref, vs_ref, xn_ref):
    del a0, a1, a2, a3
    _normed_rows(x_ref, g_ref, xn_ref)
    j = pl.program_id(1)
    y = _dot(xn_ref[...], w_ref[...])
    for idx, ref in enumerate((kd_ref, vd_ref, ks_ref, vs_ref)):
        @pl.when(j == idx)
        def _(ref=ref, idx=idx):
            for h, yh in enumerate(_kv_heads(y, kg_ref, pm_ref, idx == 0)):
                ref[h] = yh


def _kvproj_prompt(x2, norm_g, w_bf, k_gain, pmat, stacks, *, layer, seq, tm):
    rows, d = x2.shape
    width = w_bf.shape[1] // 4
    n_heads = width // HEAD_DIM
    tpb = seq // tm
    any_spec = pl.BlockSpec(memory_space=pl.ANY)
    out_spec = pl.BlockSpec((None, None, n_heads, tm, HEAD_DIM),
                            lambda i, j: (layer, i // tpb, 0, i % tpb, 0))
    return pl.pallas_call(
        _kvproj_prompt_kernel,
        grid=(rows // tm, 4),
        in_specs=[
            pl.BlockSpec((tm, d), lambda i, j: (i, 0)),
            pl.BlockSpec((1, d), lambda i, j: (0, 0)),
            pl.BlockSpec((d, width), lambda i, j: (0, j)),
            pl.BlockSpec((1, HEAD_DIM), lambda i, j: (0, 0)),
            pl.BlockSpec((LANES, LANES), lambda i, j: (0, 0)),
            any_spec, any_spec, any_spec, any_spec,
        ],
        out_specs=[out_spec] * 4,
        out_shape=[jax.ShapeDtypeStruct(s.shape, s.dtype) for s in stacks],
        input_output_aliases={5: 0, 6: 1, 7: 2, 8: 3},
        scratch_shapes=[pltpu.VMEM((tm, d), BF16)],
        compiler_params=_params(("parallel", "arbitrary")),
        name="kvproj_prompt",
    )(x2, norm_g, w_bf, k_gain, pmat, *stacks)


def _kvproj_step_kernel(x_ref, g_ref, w_ref, kg_ref, pm_ref, o_ref, xn_ref):
    _normed_rows(x_ref, g_ref, xn_ref)
    j = pl.program_id(1)
    y = _dot(xn_ref[...], w_ref[...])
    for normed in (True, False):
        @pl.when((j == 0) == normed)
        def _(normed=normed):
            for h, yh in enumerate(_kv_heads(y, kg_ref, pm_ref, normed)):
                o_ref[:, h * HEAD_DIM:(h + 1) * HEAD_DIM] = yh


def _kvproj_step(x2, norm_g, w_bf, k_gain, pmat):
    rows, d = x2.shape
    ncol = w_bf.shape[1]
    width = ncol // 4
    return pl.pallas_call(
        _kvproj_step_kernel,
        grid=(1, 4),
        in_specs=[
            pl.BlockSpec((rows, d), lambda i, j: (i, 0)),
            pl.BlockSpec((1, d), lambda i, j: (0, 0)),
            pl.BlockSpec((d, width), lambda i, j: (0, j)),
            pl.BlockSpec((1, HEAD_DIM), lambda i, j: (0, 0)),
            pl.BlockSpec((LANES, LANES), lambda i, j: (0, 0)),
        ],
        out_specs=pl.BlockSpec((rows, width), lambda i, j: (i, j)),
        out_shape=jax.ShapeDtypeStruct((rows, ncol), F32),
        scratch_shapes=[pltpu.VMEM((rows, d), BF16)],
        compiler_params=_params(("parallel", "arbitrary")),
        name="kvproj_step",
    )(x2, norm_g, w_bf, k_gain, pmat)


def _s5_prep_kernel(lr_ref, li_ref, ldt_ref, br_ref, bi_ref, ar_ref, ai_ref, bbr_ref, bbi_ref):
    lr, li = lr_ref[...], li_ref[...]
    dt = jnp.exp(ldt_ref[...])
    mag = jnp.exp(lr * dt)
    ar, ai = mag * jnp.cos(li * dt), mag * jnp.sin(li * dt)
    den = lr * lr + li * li
    kr = ((ar - 1.0) * lr + ai * li) / den
    ki = (ai * lr - (ar - 1.0) * li) / den
    br, bi = br_ref[...], bi_ref[...]
    ar_ref[...] = ar
    ai_ref[...] = ai
    bbr_ref[...] = kr * br - ki * bi
    bbi_ref[...] = kr * bi + ki * br


def _s5_prep(lam_re, lam_im, log_dt, b_re, b_im):
    dep, g, n = lam_re.shape
    p = b_re.shape[-1]
    rows = dep * g * p

    def rep(a):
        return jnp.broadcast_to(a[:, :, None, :], (dep, g, p, n)).reshape(rows, n)

    ldt = jnp.broadcast_to(log_dt[:, :, None, None], (dep, g, p, n)).reshape(rows, n)
    brt = jnp.swapaxes(b_re, 2, 3).reshape(rows, n)
    bit = jnp.swapaxes(b_im, 2, 3).reshape(rows, n)
    shp = jax.ShapeDtypeStruct((rows, n), F32)
    ar, ai, bbr, bbi = pl.pallas_call(
        _s5_prep_kernel, out_shape=(shp, shp, shp, shp), name="s5_prep",
    )(rep(lam_re), rep(lam_im), ldt, brt, bit)
    ar = ar.reshape(dep, g, p, n)[:, :, 0]
    ai = ai.reshape(dep, g, p, n)[:, :, 0]
    return ar, ai, bbr.reshape(dep, g, p, n), bbi.reshape(dep, g, p, n)


def _s5_layouts(ar, ai, bbr, bbi, c_re, c_im):
    g, n = ar.shape
    nch = g // SSM_CHUNK
    half = SSM_CHUNK * n
    eye = jnp.eye(SSM_CHUNK, dtype=F32)

    def in_bd(bb):
        bb = bb.reshape(nch, SSM_CHUNK, SSM_P, n)
        return (bb[:, :, :, None, :] * eye[None, :, None, :, None]).reshape(nch, LANES, half)

    def out_bd(c):
        c = jnp.swapaxes(c.reshape(nch, SSM_CHUNK, SSM_P, n), 2, 3)
        return (c[:, :, :, None, :] * eye[None, :, None, :, None]).reshape(nch, half, LANES)

    bbd = jnp.concatenate([in_bd(bbr), in_bd(bbi)], axis=-1).astype(BF16)
    cbd = jnp.concatenate([out_bd(c_re), -out_bd(c_im)], axis=1).astype(BF16)
    return bbd, cbd, ar.reshape(nch, 1, half), ai.reshape(nch, 1, half)


def _cmul(ar, ai, br, bi):
    return ar * br - ai * bi, ar * bi + ai * br


def _s5_prompt_kernel(u_ref, bbd_ref, cbd_ref, ar_ref, ai_ref, d_ref, y_ref, ht_ref, x_scr,
                      *, nseg, seglen, unroll):
    half = ar_ref.shape[-1]
    jc = min(seglen, 64)
    rc = jc * nseg

    def fill(c, carry):
        r0 = pl.multiple_of(c * rc, rc)
        x = _dot(u_ref[pl.ds(r0, rc), :].astype(BF16), bbd_ref[...])
        x_scr[pl.ds(c * jc, jc)] = x.reshape(jc, nseg, 2 * half)
        return carry

    lax.fori_loop(0, seglen // jc, fill, 0)
    ar = jnp.broadcast_to(ar_ref[...], (nseg, half))
    ai = jnp.broadcast_to(ai_ref[...], (nseg, half))

    def scan_step(j, carry):
        hr, hi = carry
        pr, pi = _cmul(ar, ai, hr, hi)
        hr = pr + x_scr[j, :, :half]
        hi = pi + x_scr[j, :, half:]
        x_scr[j, :, :half] = hr
        x_scr[j, :, half:] = hi
        return hr, hi

    zero = jnp.zeros((nseg, half), F32)
    er, ei = lax.fori_loop(0, seglen, scan_step, (zero, zero), unroll=unroll)

    sr, si = ar_ref[...], ai_ref[...]
    for _ in range(int(math.log2(seglen))):
        sr, si = _cmul(sr, si, sr, si)

    cin_r, cin_i = [jnp.zeros((1, half), F32)], [jnp.zeros((1, half), F32)]
    for s in range(nseg):
        pr, pi = _cmul(sr, si, cin_r[-1], cin_i[-1])
        cin_r.append(er[s:s + 1] + pr)
        cin_i.append(ei[s:s + 1] + pi)
    ht_ref[:, :half] = cin_r[-1]
    ht_ref[:, half:] = cin_i[-1]
    cr = jnp.concatenate(cin_r[:-1], axis=0)
    ci = jnp.concatenate(cin_i[:-1], axis=0)

    def fix_step(j, carry):
        pr, pi = carry
        dr, di = _cmul(pr, pi, cr, ci)
        x_scr[j, :, :half] = x_scr[j, :, :half] + dr
        x_scr[j, :, half:] = x_scr[j, :, half:] + di
        return _cmul(pr, pi, ar, ai)

    lax.fori_loop(0, seglen, fix_step, (ar, ai), unroll=unroll)

    def drain(c, carry):
        r0 = pl.multiple_of(c * rc, rc)
        h = x_scr[pl.ds(c * jc, jc)].reshape(rc, 2 * half)
        y = _dot(h.astype(BF16), cbd_ref[...]) + d_ref[...] * u_ref[pl.ds(r0, rc), :]
        y_ref[pl.ds(r0, rc), :] = jax.nn.gelu(y)
        return carry

    lax.fori_loop(0, seglen // jc, drain, 0)


def _s5_prompt(u_perm, bbd, cbd, ar_l, ai_l, d_skip, *, nseg):
    b, t, w = u_perm.shape
    nch, _, cols = bbd.shape
    half = cols // 2
    seglen = t // nseg
    kern = functools.partial(_s5_prompt_kernel, nseg=nseg, seglen=seglen, unroll=min(8, seglen))
    return pl.pallas_call(
        kern,
        grid=(b, nch),
        in_specs=[
            pl.BlockSpec((None, t, LANES), lambda i, c: (i, 0, c)),
            pl.BlockSpec((None, LANES, cols), lambda i, c: (c, 0, 0)),
            pl.BlockSpec((None, cols, LANES), lambda i, c: (c, 0, 0)),
            pl.BlockSpec((None, 1, half), lambda i, c: (c, 0, 0)),
            pl.BlockSpec((None, 1, half), lambda i, c: (c, 0, 0)),
            pl.BlockSpec((1, LANES), lambda i, c: (0, c)),
        ],
        out_specs=[
            pl.BlockSpec((None, t, LANES), lambda i, c: (i, 0, c)),
            pl.BlockSpec((None, None, 1, cols), lambda i, c: (i, c, 0, 0)),
        ],
        out_shape=[jax.ShapeDtypeStruct((b, t, w), F32),
                   jax.ShapeDtypeStruct((b, nch, 1, cols), F32)],
        scratch_shapes=[pltpu.VMEM((seglen, nseg, cols), F32)],
        compiler_params=_params(("parallel", "arbitrary")),
        name="s5_prompt",
    )(u_perm, bbd, cbd, ar_l, ai_l, d_skip)


def _s5_step_kernel(p_ref, h0_ref, bbd_ref, cbd_ref, ar_ref, ai_ref, d_ref, y_ref, ht_ref, *, nch):
    half = ar_ref.shape[-1]
    for c in range(nch):
        u = p_ref[:, c * LANES:(c + 1) * LANES]
        x = _dot(u.astype(BF16), bbd_ref[c])
        h0r, h0i = h0_ref[c, :, :half], h0_ref[c, :, half:]
        pr, pi = _cmul(ar_ref[c], ai_ref[c], h0r, h0i)
        hr = pr + x[:, :half]
        hi = pi + x[:, half:]
        ht_ref[c, :, :half] = hr
        ht_ref[c, :, half:] = hi
        h = jnp.concatenate([hr, hi], axis=-1)
        y = _dot(h.astype(BF16), cbd_ref[c]) + d_ref[:, c * LANES:(c + 1) * LANES] * u
        y_ref[:, c * LANES:(c + 1) * LANES] = jax.nn.gelu(y)


def _s5_step(proj_s, h0_l, bbd, cbd, ar_l, ai_l, d_skip):
    b = proj_s.shape[0]
    nch, _, cols = bbd.shape
    w = nch * LANES
    kern = functools.partial(_s5_step_kernel, nch=nch)
    return pl.pallas_call(
        kern,
        out_shape=[jax.ShapeDtypeStruct((b, w), F32), jax.ShapeDtypeStruct((nch, b, cols), F32)],
        compiler_params=pltpu.CompilerParams(vmem_limit_bytes=VMEM_LIMIT),
        name="s5_step",
    )(proj_s, h0_l, bbd, cbd, ar_l, ai_l, d_skip)


def _loop_pairs(n, step, carry):
    carry = lax.fori_loop(0, n // 2, lambda i, c: step(2 * i + 1, step(2 * i, c)), carry)
    return lax.cond(n % 2 == 1, lambda c: step(n - 1, c), lambda c: c, carry)


def _diff_lambda(lq1_ref, lk1_ref, lq2_ref, lk2_ref, lam_init):
    s1 = jnp.sum(lq1_ref[...] * lk1_ref[...], axis=-1, keepdims=True)
    s2 = jnp.sum(lq2_ref[...] * lk2_ref[...], axis=-1, keepdims=True)
    return jnp.exp(s1) - jnp.exp(s2) + lam_init


def _diff_prompt_kernel(q_ref, k_ref, v_ref, g_ref, lq1_ref, lk1_ref, lq2_ref, lk2_ref, sg_ref,
                        qg_ref, pm_ref, o_ref, *, blk, lam_init):
    qi = pl.program_id(2)
    q = _half_rms(q_ref[...], pm_ref[...], qg_ref[...]) * (DIFF_HALF ** -0.5)
    lane = lax.broadcasted_iota(jnp.int32, q.shape, 1)
    q2 = jnp.concatenate([jnp.where(lane < DIFF_HALF, q, 0.0),
                          jnp.where(lane >= DIFF_HALF, q, 0.0)], axis=0).astype(BF16)

    def step(ki, carry, masked):
        m, l, acc = carry
        start = pl.multiple_of(ki * blk, blk)
        k = k_ref[pl.ds(start, blk), :].astype(BF16)
        v = v_ref[pl.ds(start, blk), :].astype(BF16)
        s = _dot_nt(q2, k)
        if masked:
            row = lax.broadcasted_iota(jnp.int32, s.shape, 0)
            col = lax.broadcasted_iota(jnp.int32, s.shape, 1)
            row = jnp.where(row >= blk, row - blk, row)
            s = jnp.where(col <= row, s, -jnp.inf)
        m_new = jnp.maximum(m, jnp.max(s, axis=-1, keepdims=True))
        alpha = jnp.exp(m - m_new)
        p = jnp.exp(s - m_new)
        l = alpha * l + jnp.sum(p, axis=-1, keepdims=True)
        acc = alpha * acc + _dot(p.astype(BF16), v)
        return m_new, l, acc

    init = (jnp.full((2 * blk, 1), -jnp.inf, F32), jnp.zeros((2 * blk, 1), F32),
            jnp.zeros((2 * blk, HEAD_DIM), F32))
    carry = step(qi, init, True)
    m, l, acc = _loop_pairs(qi, lambda ki, c: step(ki, c, False), carry)

    lam = _diff_lambda(lq1_ref, lk1_ref, lq2_ref, lk2_ref, lam_init)
    o = acc[:blk] / l[:blk] - lam * (acc[blk:] / l[blk:])
    ms = jnp.mean(o * o, axis=-1, keepdims=True)
    o = o * lax.rsqrt(ms + EPS) * sg_ref[...] * (1.0 - lam_init)
    o_ref[...] = (o * _silu(g_ref[...])).astype(BF16)


def _sb_scores(z):
    sp = jnp.log1p(jnp.exp(-jnp.abs(z)))
    return jnp.minimum(z, 0.0) - sp, -jnp.maximum(z, 0.0) - sp


def _strict_lower(n):
    r = lax.broadcasted_iota(jnp.int32, (n, n), 0)
    c = lax.broadcasted_iota(jnp.int32, (n, n), 1)
    return jnp.where(r > c, 1.0, 0.0).astype(BF16)


def _sum_later(lk, tri):
    hi, lo = _split_hi_lo(lk)
    return _dot(hi, tri) + _dot(lo, tri)


def _sb_prompt_kernel(q_ref, k_ref, v_ref, g_ref, o_ref, *, blk):
    qi = pl.program_id(2)
    q = (q_ref[...] * (HEAD_DIM ** -0.5)).astype(BF16)
    tri = _strict_lower(blk)

    def step(ki, carry, masked):
        r, acc = carry
        start = pl.multiple_of(ki * blk, blk)
        k = k_ref[pl.ds(start, blk), :].astype(BF16)
        v = v_ref[pl.ds(start, blk), :].astype(BF16)
        lb, lk = _sb_scores(_dot_nt(q, k))
        if masked:
            row = lax.broadcasted_iota(jnp.int32, lb.shape, 0)
            col = lax.broadcasted_iota(jnp.int32, lb.shape, 1)
            mask = col < row
            lk = jnp.where(mask, lk, 0.0)
        a = jnp.exp(lb + (r + _sum_later(lk, tri)))
        if masked:
            a = jnp.where(mask, a, 0.0)
        acc = acc + _dot(a.astype(BF16), v)
        return r + jnp.sum(lk, axis=-1, keepdims=True), acc

    r, acc = step(qi, (jnp.zeros((blk, 1), F32), jnp.zeros((blk, HEAD_DIM), F32)), True)

    def body(c):
        ki, _, r, acc = c
        r, acc = step(ki, (r, acc), False)
        return ki - 1, jnp.max(r), r, acc

    def live(c):
        return jnp.logical_and(c[0] >= 0, c[1] > SB_EXIT_LOG)

    _, _, _, acc = lax.while_loop(live, body, (qi - 1, jnp.max(r), r, acc))
    o_ref[...] = (acc * _silu(g_ref[...])).astype(BF16)


def _prompt_attention(kern, proj3, k_stack, v_stack, extra, *, layer, blk, col_q, col_g, name):
    b, t, _ = proj3.shape
    n_heads = k_stack.shape[2]
    qspec = lambda c0: pl.BlockSpec((None, blk, HEAD_DIM), lambda i, h, q: (i, q, c0 + h))
    kvspec = pl.BlockSpec((None, None, None, t, HEAD_DIM), lambda i, h, q: (layer, i, h, 0, 0))
    small = [pl.BlockSpec(e.shape, lambda i, h, q: (0, 0)) for e in extra]
    return pl.pallas_call(
        kern,
        grid=(b, n_heads, t // blk),
        in_specs=[qspec(col_q), kvspec, kvspec, qspec(col_g)] + small,
        out_specs=pl.BlockSpec((None, blk, HEAD_DIM), lambda i, h, q: (i, q, h)),
        out_shape=jax.ShapeDtypeStruct((b, t, n_heads * HEAD_DIM), BF16),
        compiler_params=_params(("parallel", "parallel", "arbitrary")),
        name=name,
    )(proj3, k_stack, v_stack, proj3, *extra)


def _page_scores(wq_scr, k_refs, n_heads):
    return jnp.concatenate([
        jnp.concatenate([_dot_nt(wq_scr[h], k[h].astype(BF16)) for k in k_refs], axis=-1)
        for h in range(n_heads)], axis=0)


def _page_values(p, v_refs, n_heads):
    out = []
    for h in range(n_heads):
        ph = p[h * SUBLANES:(h + 1) * SUBLANES].astype(BF16)
        pv = _dot(ph[:, :LANES], v_refs[0][h].astype(BF16))
        for j in range(1, len(v_refs)):
            pv = pv + _dot(ph[:, j * LANES:(j + 1) * LANES], v_refs[j][h].astype(BF16))
        out.append(pv)
    return jnp.concatenate(out, axis=0)


def _diff_decode_kernel(pt_ref, p_ref, kv_ref, lq1_ref, lk1_ref, lq2_ref, lk2_ref, sg_ref, qg_ref,
                        pm_ref, *rest, n_pp, col_q, col_g, n_heads, lam_init):
    del pt_ref
    k_refs, v_refs = rest[:n_pp], rest[n_pp:2 * n_pp]
    o_ref, wq_scr, m_scr, l_scr, acc_scr = rest[2 * n_pp:]
    width = n_heads * HEAD_DIM
    s_idx = pl.program_id(1)

    @pl.when(s_idx == 0)
    def _():
        row = lax.broadcasted_iota(jnp.int32, (SUBLANES, HEAD_DIM), 0)
        lane = lax.broadcasted_iota(jnp.int32, (SUBLANES, HEAD_DIM), 1)
        keep = jnp.logical_or(jnp.logical_and(row == 0, lane < DIFF_HALF),
                              jnp.logical_and(row == 1, lane >= DIFF_HALF))
        for h in range(n_heads):
            qh = p_ref[:, col_q + h * HEAD_DIM:col_q + (h + 1) * HEAD_DIM]
            qh = _half_rms(jnp.broadcast_to(qh, (SUBLANES, HEAD_DIM)), pm_ref[...], qg_ref[...])
            wq_scr[h] = jnp.where(keep, qh * (DIFF_HALF ** -0.5), 0.0).astype(BF16)
        m_scr[...] = jnp.full(m_scr.shape, -jnp.inf, F32)
        l_scr[...] = jnp.zeros(l_scr.shape, F32)
        acc_scr[...] = jnp.zeros(acc_scr.shape, F32)

    s = _page_scores(wq_scr, k_refs, n_heads)
    m_old = m_scr[...]
    m_new = jnp.maximum(m_old, jnp.max(s, axis=-1, keepdims=True))
    alpha = jnp.exp(m_old - m_new)
    p = jnp.exp(s - m_new)
    l_scr[...] = alpha * l_scr[...] + jnp.sum(p, axis=-1, keepdims=True)
    acc_scr[...] = alpha * acc_scr[...] + _page_values(p, v_refs, n_heads)
    m_scr[...] = m_new

    @pl.when(s_idx == pl.num_programs(1) - 1)
    def _():
        lam = _diff_lambda(lq1_ref, lk1_ref, lq2_ref, lk2_ref, lam_init)
        outs = []
        for h in range(n_heads):
            rows = slice(h * SUBLANES, (h + 1) * SUBLANES)
            k_new = kv_ref[:, h * HEAD_DIM:(h + 1) * HEAD_DIM].astype(BF16).astype(F32)
            v_new = kv_ref[:, width + h * HEAD_DIM:width + (h + 1) * HEAD_DIM].astype(BF16).astype(F32)
            s_new = jnp.sum(wq_scr[h].astype(F32) * k_new, axis=-1, keepdims=True)
            m_old = m_scr[rows]
            m_fin = jnp.maximum(m_old, s_new)
            a_fin = jnp.exp(m_old - m_fin)
            p_new = jnp.exp(s_new - m_fin)
            l_fin = a_fin * l_scr[rows] + p_new
            on = (a_fin * acc_scr[rows] + p_new.astype(BF16).astype(F32) * v_new) / l_fin
            o = on[0:1] - lam * on[1:2]
            ms = jnp.mean(o * o, axis=-1, keepdims=True)
            o = o * lax.rsqrt(ms + EPS) * sg_ref[...] * (1.0 - lam_init)
            g = p_ref[:, col_g + h * HEAD_DIM:col_g + (h + 1) * HEAD_DIM]
            outs.append(o * _silu(g))
        o_ref[...] = jnp.concatenate(outs, axis=-1).astype(BF16)


def _sb_decode_kernel(pt_ref, p_ref, *rest, n_pp, col_q, col_g, n_heads):
    del pt_ref
    k_refs, v_refs = rest[:n_pp], rest[n_pp:2 * n_pp]
    o_ref, wq_scr, r_scr, acc_scr = rest[2 * n_pp:]
    s_idx = pl.program_id(1)

    @pl.when(s_idx == 0)
    def _():
        for h in range(n_heads):
            qh = p_ref[:, col_q + h * HEAD_DIM:col_q + (h + 1) * HEAD_DIM] * (HEAD_DIM ** -0.5)
            wq_scr[h] = jnp.broadcast_to(qh, (SUBLANES, HEAD_DIM)).astype(BF16)
        r_scr[...] = jnp.zeros(r_scr.shape, F32)
        acc_scr[...] = jnp.zeros(acc_scr.shape, F32)

    @pl.when(jnp.max(r_scr[...]) > SB_EXIT_LOG)
    def _():
        lb, lk = _sb_scores(_page_scores(wq_scr, k_refs, n_heads))
        hi, lo = _split_hi_lo(lk)
        r_i = lax.broadcasted_iota(jnp.int32, (LANES, 2 * LANES), 0)
        c_i = lax.broadcasted_iota(jnp.int32, (LANES, 2 * LANES), 1)
        tri_ones = jnp.where(jnp.logical_or(c_i >= LANES, r_i > c_i), 1.0, 0.0).astype(BF16)
        r = r_scr[...]
        weights = []
        for j in range(n_pp):
            sl = slice(j * LANES, (j + 1) * LANES)
            sums = _dot(hi[:, sl], tri_ones) + _dot(lo[:, sl], tri_ones)
            weights.append(jnp.exp(lb[:, sl] + (r + sums[:, :LANES])))
            r = r + sums[:, LANES:]
        r_scr[...] = r
        acc_scr[...] = acc_scr[...] + _page_values(jnp.concatenate(weights, axis=-1), v_refs, n_heads)

    @pl.when(s_idx == pl.num_programs(1) - 1)
    def _():
        outs = []
        for h in range(n_heads):
            g = p_ref[:, col_g + h * HEAD_DIM:col_g + (h + 1) * HEAD_DIM]
            outs.append(acc_scr[h * SUBLANES:h * SUBLANES + 1] * _silu(g))
        o_ref[...] = jnp.concatenate(outs, axis=-1).astype(BF16)


def _decode_attention(kern, page_table, rows3, cache_k, cache_v, extra, scratch, *, layer,
                      n_pp, reverse, name):
    b = rows3[0].shape[0]
    n_pages = page_table.shape[1]
    _, _, n_heads, page, _ = cache_k.shape
    n_steps = n_pages // n_pp

    def page_spec(j):
        def imap(i, s, pt):
            slot = s * n_pp + j
            if reverse:
                slot = n_pages - 1 - slot
            return (layer, pt[i, slot], 0, 0, 0)
        return pl.BlockSpec((None, None, n_heads, page, HEAD_DIM), imap)

    per_seq = [pl.BlockSpec((None, 1, r.shape[2]), lambda i, s, pt: (i, 0, 0)) for r in rows3]
    small = [pl.BlockSpec(e.shape, lambda i, s, pt: (0, 0)) for e in extra]
    pages = [page_spec(j) for j in range(n_pp)]
    grid_spec = pltpu.PrefetchScalarGridSpec(
        num_scalar_prefetch=1,
        grid=(b, n_steps),
        in_specs=per_seq + small + pages + pages,
        out_specs=pl.BlockSpec((None, 1, n_heads * HEAD_DIM), lambda i, s, pt: (i, 0, 0)),
        scratch_shapes=scratch,
    )
    return pl.pallas_call(
        kern,
        grid_spec=grid_spec,
        out_shape=jax.ShapeDtypeStruct((b, 1, n_heads * HEAD_DIM), BF16),
        compiler_params=_params(("parallel", "arbitrary")),
        name=name,
    )(page_table, *rows3, *extra, *([cache_k] * n_pp), *([cache_v] * n_pp))


def _outproj_kernel(y_ref, ga_ref, od_ref, os_ref, x_ref, wg_ref, wa_ref, wd_ref, ws_ref, o_ref):
    y = y_ref[...]
    gate = 1.0 / (1.0 + jnp.exp(-_dot(y.astype(BF16), wg_ref[...])))
    oa = (y * gate * _silu(ga_ref[...])).astype(BF16)
    acc = _dot(oa, wa_ref[...]) + _dot(od_ref[...], wd_ref[...]) + _dot(os_ref[...], ws_ref[...])
    o_ref[...] = x_ref[...] + acc


def _outproj(y2, proj, o_d, o_s, x2, w_glu, w_a, w_d, w_s, *, tm, ga_block):
    rows, d = x2.shape
    wa = y2.shape[1]
    wd, ws = o_d.shape[1], o_s.shape[1]
    row = lambda w: pl.BlockSpec((tm, w), lambda i: (i, 0))
    full = lambda a: pl.BlockSpec(a.shape, lambda i: (0, 0))
    return pl.pallas_call(
        _outproj_kernel,
        grid=(rows // tm,),
        in_specs=[row(wa), pl.BlockSpec((tm, wa), lambda i: (i, ga_block)), row(wd), row(ws), row(d),
                  full(w_glu), full(w_a), full(w_d), full(w_s)],
        out_specs=row(d),
        out_shape=jax.ShapeDtypeStruct((rows, d), F32),
        compiler_params=_params(("parallel",)),
        name="outproj",
    )(y2, proj, o_d, o_s, x2, w_glu, w_a, w_d, w_s)


def _largest_tile(n, cap):
    t = min(n, cap)
    while n % t:
        t //= 2
    return t


def kernel(x_prompt, x_sample, cache_k_diff, cache_v_diff, cache_k_sb, cache_v_sb, state_ssm_re, state_ssm_im, page_table, norm_g, w_in, ssm_lam_re, ssm_lam_im, ssm_log_dt, ssm_b_re, ssm_b_im, ssm_c_re, ssm_c_im, ssm_d, w_glu, diff_qn_g, diff_kn_g, diff_lq1, diff_lk1, diff_lq2, diff_lk2, diff_subln_g, w_out):
    bp, t, d = x_prompt.shape
    bs = x_sample.shape[0]
    depth = w_in.shape[0]
    dh, sh = cache_k_diff.shape[3], cache_k_sb.shape[3]
    g, n = ssm_lam_re.shape[1], ssm_lam_re.shape[2]
    w_ssm = g * SSM_P
    w_diff, w_sb = dh * HEAD_DIM, sh * HEAD_DIM
    assert w_diff == w_sb
    nch = w_ssm // LANES
    half = SSM_CHUNK * n
    widths = [w_ssm, w_ssm] + [w_diff] * 4 + [w_sb] * 4
    offs = [0]
    for wd_ in widths:
        offs.append(offs[-1] + wd_)
    (r_u, r_ga, r_qd, r_kd, r_vd, r_gd, r_qs, r_ks, r_vs, r_gs, ncol) = offs
    assert ncol == w_in.shape[2]
    c_ga, c_qd, c_gd, c_qs, c_gs = w_ssm, 2 * w_ssm, 2 * w_ssm + w_diff, 2 * w_ssm + 2 * w_diff, \
        2 * w_ssm + 2 * w_diff + w_sb
    cols_a = c_gs + w_sb

    tn = 512
    assert cols_a % tn == 0
    tm_p = _largest_tile(t, 1024)
    tmo_p = _largest_tile(bp * t, 512)
    blk = _largest_tile(t, 256)
    nseg = SUBLANES
    n_pp = _largest_tile(page_table.shape[1], 8)

    lane_half = jnp.arange(LANES) // DIFF_HALF
    pmat = (lane_half[:, None] == lane_half[None, :]).astype(BF16) * (1.0 / DIFF_HALF)

    ar_all, ai_all, bbr_all, bbi_all = _s5_prep(ssm_lam_re, ssm_lam_im, ssm_log_dt, ssm_b_re, ssm_b_im)

    ck_d, cv_d, ck_s, cv_s = [jnp.transpose(c, (0, 1, 3, 2, 4))
                              for c in (cache_k_diff, cache_v_diff, cache_k_sb, cache_v_sb)]
    stacks = [jnp.zeros((depth, bp, dh, t, HEAD_DIM), F32), jnp.zeros((depth, bp, dh, t, HEAD_DIM), F32),
              jnp.zeros((depth, bp, sh, t, HEAD_DIM), F32), jnp.zeros((depth, bp, sh, t, HEAD_DIM), F32)]

    xp = x_prompt.reshape(bp * t, d)
    xs = x_sample.reshape(bs, d)
    ssm_p, rows_s = [], []
    for l in range(depth):
        lam_init = 0.8 - 0.6 * math.exp(-0.3 * l)
        wl = w_in[l]
        w_a_in = jnp.concatenate([wl[:, r_u:r_kd], wl[:, r_gd:r_ks], wl[:, r_gs:]], axis=1).astype(BF16)
        w_b_in = jnp.concatenate([wl[:, r_kd:r_gd], wl[:, r_ks:r_gs]], axis=1).astype(BF16)
        wo = w_out[l].astype(BF16)
        w_a, w_d, w_s = wo[:w_ssm], wo[w_ssm:w_ssm + w_diff], wo[w_ssm + w_diff:]
        wg = w_glu[l].astype(BF16)
        q_gain = jnp.tile(diff_qn_g[l], 2).reshape(1, HEAD_DIM)
        k_gain = jnp.tile(diff_kn_g[l], 2).reshape(1, HEAD_DIM)
        ng = norm_g[l].reshape(1, d)
        bbd, cbd, ar_l, ai_l = _s5_layouts(ar_all[l], ai_all[l], bbr_all[l], bbi_all[l],
                                           ssm_c_re[l], ssm_c_im[l])
        d_skip = ssm_d[l].reshape(1, w_ssm)
        lam_vecs = [v[l].reshape(1, DIFF_HALF) for v in (diff_lq1, diff_lk1, diff_lq2, diff_lk2)]
        diff_extra = lam_vecs + [diff_subln_g[l].reshape(1, HEAD_DIM), q_gain, pmat]

        proj = _proj(xp, ng, w_a_in, tm=tm_p, tn=tn)
        stacks = _kvproj_prompt(xp, ng, w_b_in, k_gain, pmat, stacks, layer=l, seq=t, tm=tm_p)
        proj3 = proj.reshape(bp, t, cols_a)
        u = proj3[:, :, :w_ssm].reshape(bp, nseg, t // nseg, w_ssm)
        u_perm = jnp.swapaxes(u, 1, 2).reshape(bp, t, w_ssm)
        y2p, ht = _s5_prompt(u_perm, bbd, cbd, ar_l, ai_l, d_skip, nseg=nseg)
        y2 = jnp.swapaxes(y2p.reshape(bp, t // nseg, nseg, w_ssm), 1, 2).reshape(bp * t, w_ssm)
        o_d = _prompt_attention(
            functools.partial(_diff_prompt_kernel, blk=blk, lam_init=lam_init), proj3,
            stacks[0], stacks[1], diff_extra, layer=l, blk=blk,
            col_q=c_qd // HEAD_DIM, col_g=c_gd // HEAD_DIM, name="diff_prompt")
        o_s = _prompt_attention(
            functools.partial(_sb_prompt_kernel, blk=blk), proj3, stacks[2], stacks[3], [],
            layer=l, blk=blk, col_q=c_qs // HEAD_DIM, col_g=c_gs // HEAD_DIM, name="sb_prompt")
        xp = _outproj(y2, proj, o_d.reshape(bp * t, w_diff), o_s.reshape(bp * t, w_sb), xp,
                      wg, w_a, w_d, w_s, tm=tmo_p, ga_block=c_ga // w_ssm)
        ht = ht.reshape(bp, nch, 2, SSM_CHUNK, n)
        ssm_p.append((ht[:, :, 0].reshape(bp, g, n), ht[:, :, 1].reshape(bp, g, n)))

        proj_s = _proj(xs, ng, w_a_in, tm=bs, tn=tn)
        kv_s = _kvproj_step(xs, ng, w_b_in, k_gain, pmat)
        proj_s3 = proj_s.reshape(bs, 1, cols_a)
        kv_s3 = kv_s.reshape(bs, 1, 4 * w_diff)
        h0 = jnp.concatenate([state_ssm_re[l].reshape(bs, nch, half),
                              state_ssm_im[l].reshape(bs, nch, half)], axis=-1)
        y2s, hts = _s5_step(proj_s, jnp.swapaxes(h0, 0, 1), bbd, cbd, ar_l, ai_l, d_skip)
        od_s = _decode_attention(
            functools.partial(_diff_decode_kernel, n_pp=n_pp, col_q=c_qd, col_g=c_gd, n_heads=dh,
                              lam_init=lam_init),
            page_table, [proj_s3, kv_s3[:, :, :2 * w_diff]], ck_d, cv_d, diff_extra,
            [pltpu.VMEM((dh, SUBLANES, HEAD_DIM), BF16), pltpu.VMEM((dh * SUBLANES, 1), F32),
             pltpu.VMEM((dh * SUBLANES, 1), F32), pltpu.VMEM((dh * SUBLANES, HEAD_DIM), F32)],
            layer=l, n_pp=n_pp, reverse=False, name="diff_decode")
        os_s = _decode_attention(
            functools.partial(_sb_decode_kernel, n_pp=n_pp, col_q=c_qs, col_g=c_gs, n_heads=sh),
            page_table, [proj_s3], ck_s, cv_s, [],
            [pltpu.VMEM((sh, SUBLANES, HEAD_DIM), BF16), pltpu.VMEM((sh * SUBLANES, LANES), F32),
             pltpu.VMEM((sh * SUBLANES, HEAD_DIM), F32)],
            layer=l, n_pp=n_pp, reverse=True, name="sb_decode")
        xs = _outproj(y2s, proj_s, od_s.reshape(bs, w_diff), os_s.reshape(bs, w_sb), xs,
                      wg, w_a, w_d, w_s, tm=bs, ga_block=c_ga // w_ssm)
        hts = jnp.swapaxes(hts, 0, 1).reshape(bs, nch, 2, SSM_CHUNK, n)
        kv4 = kv_s.reshape(bs, 1, 4, dh, HEAD_DIM)
        rows_s.append((kv4[:, :, 0], kv4[:, :, 1], kv4[:, :, 2], kv4[:, :, 3],
                       hts[:, :, 0].reshape(bs, g, n), hts[:, :, 1].reshape(bs, g, n)))

    kv_p = [jnp.transpose(s, (0, 1, 3, 2, 4)) for s in stacks]
    ssm_p = [jnp.stack(r) for r in zip(*ssm_p)]
    outs_s = [jnp.stack(r) for r in zip(*rows_s)]
    return (xp.reshape(bp, t, d), xs.reshape(bs, 1, d), *kv_p, *ssm_p, *outs_s)
```

```python
import functools
import math

import jax
import jax.numpy as jnp
from jax import lax
from jax.experimental import pallas as pl
from jax.experimental.pallas import tpu as pltpu

F32 = jnp.float32
BF16 = jnp.bfloat16

EPS = 1e-6
LANES = 128
SUBLANES = 8
HEAD_DIM = 128
DIFF_HALF = HEAD_DIM // 2
SSM_P = 16
SSM_N = 64
SSM_CHUNK = LANES // SSM_P
SB_EXIT_LOG = -110.0
FLASH_UNROLL = 8
LOG2_E = 1.4426950408889634
VMEM_LIMIT = 56 * 1024 * 1024


def _params(sem):
    return pltpu.CompilerParams(dimension_semantics=sem, vmem_limit_bytes=VMEM_LIMIT)


def _split_hi_lo(t):
    hi = t.astype(BF16)
    lo = (t - hi.astype(F32)).astype(BF16)
    return hi, lo


def _dot(a, b):
    return jnp.dot(a, b, preferred_element_type=F32)


def _dot_nt(a, b):
    return lax.dot_general(a, b, (((1,), (1,)), ((), ())), preferred_element_type=F32)


def _silu(x):
    return x * (1.0 / (1.0 + jnp.exp(-x)))


def _half_rms(y, pm, gain):
    hi, lo = _split_hi_lo(y * y)
    ms = _dot(hi, pm) + _dot(lo, pm)
    return y * lax.rsqrt(ms + EPS) * gain


def _normed_rows(x_ref, g_ref, xn_ref):
    @pl.when(pl.program_id(1) == 0)
    def _():
        x = x_ref[...]
        ms = jnp.mean(x * x, axis=-1, keepdims=True)
        xn_ref[...] = (x * lax.rsqrt(ms + EPS) * g_ref[...]).astype(BF16)


def _proj_kernel(x_ref, g_ref, w_ref, o_ref, xn_ref):
    _normed_rows(x_ref, g_ref, xn_ref)
    o_ref[...] = _dot(xn_ref[...], w_ref[...])


def _proj(x2, norm_g, w_bf, *, tm, tn):
    rows, d = x2.shape
    ncol = w_bf.shape[1]
    return pl.pallas_call(
        _proj_kernel,
        grid=(rows // tm, ncol // tn),
        in_specs=[
            pl.BlockSpec((tm, d), lambda i, j: (i, 0)),
            pl.BlockSpec((1, d), lambda i, j: (0, 0)),
            pl.BlockSpec((d, tn), lambda i, j: (0, j)),
        ],
        out_specs=pl.BlockSpec((tm, tn), lambda i, j: (i, j)),
        out_shape=jax.ShapeDtypeStruct((rows, ncol), F32),
        scratch_shapes=[pltpu.VMEM((tm, d), BF16)],
        compiler_params=_params(("parallel", "arbitrary")),
        name="proj",
    )(x2, norm_g, w_bf)


def _kv_heads(y, kg_ref, pm_ref, normed):
    heads = [y[:, h * HEAD_DIM:(h + 1) * HEAD_DIM] for h in range(y.shape[1] // HEAD_DIM)]
    if normed:
        heads = [_half_rms(yh, pm_ref[...], kg_ref[...]) for yh in heads]
    return heads


def _kvproj_prompt_kernel(x_ref, g_ref, w_ref, kg_ref, pm_ref, a0, a1, a2, a3,
                          kd_ref, vd_ref, ks_ref, vs_ref, xn_ref):
    del a0, a1, a2, a3
    _normed_rows(x_ref, g_ref, xn_ref)
    j = pl.program_id(1)
    y = _dot(xn_ref[...], w_ref[...])
    for idx, ref in enumerate((kd_ref, vd_ref, ks_ref, vs_ref)):
        @pl.when(j == idx)
        def _(ref=ref, idx=idx):
            for h, yh in enumerate(_kv_heads(y, kg_ref, pm_ref, idx == 0)):
                ref[h] = yh


def _kvproj_prompt(x2, norm_g, w_bf, k_gain, pmat, stacks, *, layer, seq, tm):
    rows, d = x2.shape
    width = w_bf.shape[1] // 4
    n_heads = width // HEAD_DIM
    tpb = seq // tm
    any_spec = pl.BlockSpec(memory_space=pl.ANY)
    out_spec = pl.BlockSpec((None, None, n_heads, tm, HEAD_DIM),
                            lambda i, j: (layer, i // tpb, 0, i % tpb, 0))
    return pl.pallas_call(
        _kvproj_prompt_kernel,
        grid=(rows // tm, 4),
        in_specs=[
            pl.BlockSpec((tm, d), lambda i, j: (i, 0)),
            pl.BlockSpec((1, d), lambda i, j: (0, 0)),
            pl.BlockSpec((d, width), lambda i, j: (0, j)),
            pl.BlockSpec((1, HEAD_DIM), lambda i, j: (0, 0)),
            pl.BlockSpec((LANES, LANES), lambda i, j: (0, 0)),
            any_spec, any_spec, any_spec, any_spec,
        ],
        out_specs=[out_spec] * 4,
        out_shape=[jax.ShapeDtypeStruct(s.shape, s.dtype) for s in stacks],
        input_output_aliases={5: 0, 6: 1, 7: 2, 8: 3},
        scratch_shapes=[pltpu.VMEM((tm, d), BF16)],
        compiler_params=_params(("parallel", "arbitrary")),
        name="kvproj_prompt",
    )(x2, norm_g, w_bf, k_gain, pmat, *stacks)


def _kvproj_step_kernel(x_ref, g_ref, w_ref, kg_ref, pm_ref, o_ref, xn_ref):
    _normed_rows(x_ref, g_ref, xn_ref)
    j = pl.program_id(1)
    y = _dot(xn_ref[...], w_ref[...])
    for normed in (True, False):
        @pl.when((j == 0) == normed)
        def _(normed=normed):
            for h, yh in enumerate(_kv_heads(y, kg_ref, pm_ref, normed)):
                o_ref[:, h * HEAD_DIM:(h + 1) * HEAD_DIM] = yh


def _kvproj_step(x2, norm_g, w_bf, k_gain, pmat):
    rows, d = x2.shape
    ncol = w_bf.shape[1]
    width = ncol // 4
    return pl.pallas_call(
        _kvproj_step_kernel,
        grid=(1, 4),
        in_specs=[
            pl.BlockSpec((rows, d), lambda i, j: (i, 0)),
            pl.BlockSpec((1, d), lambda i, j: (0, 0)),
            pl.BlockSpec((d, width), lambda i, j: (0, j)),
            pl.BlockSpec((1, HEAD_DIM), lambda i, j: (0, 0)),
            pl.BlockSpec((LANES, LANES), lambda i, j: (0, 0)),
        ],
        out_specs=pl.BlockSpec((rows, width), lambda i, j: (i, j)),
        out_shape=jax.ShapeDtypeStruct((rows, ncol), F32),
        scratch_shapes=[pltpu.VMEM((rows, d), BF16)],
        compiler_params=_params(("parallel", "arbitrary")),
        name="kvproj_step",
    )(x2, norm_g, w_bf, k_gain, pmat)


def _s5_prep_kernel(lr_ref, li_ref, ldt_ref, br_ref, bi_ref, ar_ref, ai_ref, bbr_ref, bbi_ref):
    lr, li = lr_ref[...], li_ref[...]
    dt = jnp.exp(ldt_ref[...])
    mag = jnp.exp(lr * dt)
    ar, ai = mag * jnp.cos(li * dt), mag * jnp.sin(li * dt)
    den = lr * lr + li * li
    kr = ((ar - 1.0) * lr + ai * li) / den
    ki = (ai * lr - (ar - 1.0) * li) / den
    br, bi = br_ref[...], bi_ref[...]
    ar_ref[...] = ar
    ai_ref[...] = ai
    bbr_ref[...] = kr * br - ki * bi
    bbi_ref[...] = kr * bi + ki * br


def _s5_prep(lam_re, lam_im, log_dt, b_re, b_im):
    dep, g, n = lam_re.shape
    p = b_re.shape[-1]
    rows = dep * g * p

    def rep(a):
        return jnp.broadcast_to(a[:, :, None, :], (dep, g, p, n)).reshape(rows, n)

    ldt = jnp.broadcast_to(log_dt[:, :, None, None], (dep, g, p, n)).reshape(rows, n)
    brt = jnp.swapaxes(b_re, 2, 3).reshape(rows, n)
    bit = jnp.swapaxes(b_im, 2, 3).reshape(rows, n)
    shp = jax.ShapeDtypeStruct((rows, n), F32)
    ar, ai, bbr, bbi = pl.pallas_call(
        _s5_prep_kernel, out_shape=(shp, shp, shp, shp), name="s5_prep",
    )(rep(lam_re), rep(lam_im), ldt, brt, bit)
    ar = ar.reshape(dep, g, p, n)[:, :, 0]
    ai = ai.reshape(dep, g, p, n)[:, :, 0]
    return ar, ai, bbr.reshape(dep, g, p, n), bbi.reshape(dep, g, p, n)


def _s5_layouts(ar, ai, bbr, bbi, c_re, c_im):
    g, n = ar.shape
    nch = g // SSM_CHUNK
    half = SSM_CHUNK * n
    eye = jnp.eye(SSM_CHUNK, dtype=F32)

    def in_bd(bb):
        bb = bb.reshape(nch, SSM_CHUNK, SSM_P, n)
        return (bb[:, :, :, None, :] * eye[None, :, None, :, None]).reshape(nch, LANES, half)

    def out_bd(c):
        c = jnp.swapaxes(c.reshape(nch, SSM_CHUNK, SSM_P, n), 2, 3)
        return (c[:, :, :, None, :] * eye[None, :, None, :, None]).reshape(nch, half, LANES)

    bbd = jnp.concatenate([in_bd(bbr), in_bd(bbi)], axis=-1).astype(BF16)
    cbd = jnp.concatenate([out_bd(c_re), -out_bd(c_im)], axis=1).astype(BF16)
    return bbd, cbd, ar.reshape(nch, 1, half), ai.reshape(nch, 1, half)


def _cmul(ar, ai, br, bi):
    return ar * br - ai * bi, ar * bi + ai * br


def _s5_prompt_kernel(u_ref, bbd_ref, cbd_ref, ar_ref, ai_ref, d_ref, y_ref, ht_ref, x_scr,
                      *, nseg, seglen, unroll):
    half = ar_ref.shape[-1]
    jc = min(seglen, 64)
    rc = jc * nseg

    def fill(c, carry):
        r0 = pl.multiple_of(c * rc, rc)
        x = _dot(u_ref[pl.ds(r0, rc), :].astype(BF16), bbd_ref[...])
        x_scr[pl.ds(c * jc, jc)] = x.reshape(jc, nseg, 2 * half)
        return carry

    lax.fori_loop(0, seglen // jc, fill, 0)
    ar = jnp.broadcast_to(ar_ref[...], (nseg, half))
    ai = jnp.broadcast_to(ai_ref[...], (nseg, half))

    def scan_step(j, carry):
        hr, hi = carry
        pr, pi = _cmul(ar, ai, hr, hi)
        hr = pr + x_scr[j, :, :half]
        hi = pi + x_scr[j, :, half:]
        x_scr[j, :, :half] = hr
        x_scr[j, :, half:] = hi
        return hr, hi

    zero = jnp.zeros((nseg, half), F32)
    er, ei = lax.fori_loop(0, seglen, scan_step, (zero, zero), unroll=unroll)

    sr, si = ar_ref[...], ai_ref[...]
    for _ in range(int(math.log2(seglen))):
        sr, si = _cmul(sr, si, sr, si)

    cin_r, cin_i = [jnp.zeros((1, half), F32)], [jnp.zeros((1, half), F32)]
    for s in range(nseg):
        pr, pi = _cmul(sr, si, cin_r[-1], cin_i[-1])
        cin_r.append(er[s:s + 1] + pr)
        cin_i.append(ei[s:s + 1] + pi)
    ht_ref[:, :half] = cin_r[-1]
    ht_ref[:, half:] = cin_i[-1]
    cr = jnp.concatenate(cin_r[:-1], axis=0)
    ci = jnp.concatenate(cin_i[:-1], axis=0)

    def fix_step(j, carry):
        pr, pi = carry
        dr, di = _cmul(pr, pi, cr, ci)
        x_scr[j, :, :half] = x_scr[j, :, :half] + dr
        x_scr[j, :, half:] = x_scr[j, :, half:] + di
        return _cmul(pr, pi, ar, ai)

    lax.fori_loop(0, seglen, fix_step, (ar, ai), unroll=unroll)

    def drain(c, carry):
        r0 = pl.multiple_of(c * rc, rc)
        h = x_scr[pl.ds(c * jc, jc)].reshape(rc, 2 * half)
        y = _dot(h.astype(BF16), cbd_ref[...]) + d_ref[...] * u_ref[pl.ds(r0, rc), :]
        y_ref[pl.ds(r0, rc), :] = jax.nn.gelu(y)
        return carry

    lax.fori_loop(0, seglen // jc, drain, 0)


def _s5_prompt(u_perm, bbd, cbd, ar_l, ai_l, d_skip, *, nseg):
    b, t, w = u_perm.shape
    nch, _, cols = bbd.shape
    half = cols // 2
    seglen = t // nseg
    kern = functools.partial(_s5_prompt_kernel, nseg=nseg, seglen=seglen, unroll=min(8, seglen))
    return pl.pallas_call(
        kern,
        grid=(b, nch),
        in_specs=[
            pl.BlockSpec((None, t, LANES), lambda i, c: (i, 0, c)),
            pl.BlockSpec((None, LANES, cols), lambda i, c: (c, 0, 0)),
            pl.BlockSpec((None, cols, LANES), lambda i, c: (c, 0, 0)),
            pl.BlockSpec((None, 1, half), lambda i, c: (c, 0, 0)),
            pl.BlockSpec((None, 1, half), lambda i, c: (c, 0, 0)),
            pl.BlockSpec((1, LANES), lambda i, c: (0, c)),
        ],
        out_specs=[
            pl.BlockSpec((None, t, LANES), lambda i, c: (i, 0, c)),
            pl.BlockSpec((None, None, 1, cols), lambda i, c: (i, c, 0, 0)),
        ],
        out_shape=[jax.ShapeDtypeStruct((b, t, w), F32),
                   jax.ShapeDtypeStruct((b, nch, 1, cols), F32)],
        scratch_shapes=[pltpu.VMEM((seglen, nseg, cols), F32)],
        compiler_params=_params(("parallel", "arbitrary")),
        name="s5_prompt",
    )(u_perm, bbd, cbd, ar_l, ai_l, d_skip)


def _s5_step_kernel(p_ref, h0_ref, bbd_ref, cbd_ref, ar_ref, ai_ref, d_ref, y_ref, ht_ref, *, nch):
    half = ar_ref.shape[-1]
    for c in range(nch):
        u = p_ref[:, c * LANES:(c + 1) * LANES]
        x = _dot(u.astype(BF16), bbd_ref[c])
        h0r, h0i = h0_ref[c, :, :half], h0_ref[c, :, half:]
        pr, pi = _cmul(ar_ref[c], ai_ref[c], h0r, h0i)
        hr = pr + x[:, :half]
        hi = pi + x[:, half:]
        ht_ref[c, :, :half] = hr
        ht_ref[c, :, half:] = hi
        h = jnp.concatenate([hr, hi], axis=-1)
        y = _dot(h.astype(BF16), cbd_ref[c]) + d_ref[:, c * LANES:(c + 1) * LANES] * u
        y_ref[:, c * LANES:(c + 1) * LANES] = jax.nn.gelu(y)


def _s5_step(proj_s, h0_l, bbd, cbd, ar_l, ai_l, d_skip):
    b = proj_s.shape[0]
    nch, _, cols = bbd.shape
    w = nch * LANES
    kern = functools.partial(_s5_step_kernel, nch=nch)
    return pl.pallas_call(
        kern,
        out_shape=[jax.ShapeDtypeStruct((b, w), F32), jax.ShapeDtypeStruct((nch, b, cols), F32)],
        compiler_params=pltpu.CompilerParams(vmem_limit_bytes=VMEM_LIMIT),
        name="s5_step",
    )(proj_s, h0_l, bbd, cbd, ar_l, ai_l, d_skip)


def _run_blocks(ks, scores, update, carry):
    s_next = scores(ks[0])
    for idx, k in enumerate(ks):
        s_cur = s_next
        if idx + 1 < len(ks):
            s_next = scores(ks[idx + 1])
        carry = update(k, s_cur, carry)
    return carry


def _loop_blocks(n, scores, update, carry, unroll):
    carry = lax.fori_loop(
        0, n // unroll,
        lambda i, c: _run_blocks([unroll * i + u for u in range(unroll)], scores, update, c), carry)
    base = (n // unroll) * unroll
    rem = n - base
    width = unroll // 2
    while width >= 1:
        take = (rem & width) != 0
        carry = lax.cond(
            take,
            lambda c, base=base, width=width: _run_blocks([base + u for u in range(width)],
                                                          scores, update, c),
            lambda c: c, carry)
        base = base + jnp.where(take, width, 0)
        width //= 2
    return carry


def _diff_lambda(lq1_ref, lk1_ref, lq2_ref, lk2_ref, lam_init):
    s1 = jnp.sum(lq1_ref[...] * lk1_ref[...], axis=-1, keepdims=True)
    s2 = jnp.sum(lq2_ref[...] * lk2_ref[...], axis=-1, keepdims=True)
    return jnp.exp(s1) - jnp.exp(s2) + lam_init


def _diff_prompt_kernel(q_ref, k_ref, v_ref, g_ref, lq1_ref, lk1_ref, lq2_ref, lk2_ref, sg_ref,
                        qg_ref, pm_ref, o_ref, vt_scr, *, blk, lam_init):
    qi = pl.program_id(2)

    @pl.when(qi == 0)
    def _():
        for c in range(vt_scr.shape[0]):
            vt_scr[c] = v_ref[c * blk:(c + 1) * blk, :].T.astype(BF16)

    q = _half_rms(q_ref[...], pm_ref[...], qg_ref[...]) * (DIFF_HALF ** -0.5 * LOG2_E)
    lane = lax.broadcasted_iota(jnp.int32, q.shape, 1)
    q2 = jnp.concatenate([jnp.where(lane < DIFF_HALF, q, 0.0),
                          jnp.where(lane >= DIFF_HALF, q, 0.0)], axis=0).astype(BF16)

    def scores(ki):
        start = pl.multiple_of(ki * blk, blk)
        return _dot_nt(k_ref[pl.ds(start, blk), :].astype(BF16), q2)

    def update(ki, s, carry):
        m, l, acc = carry
        m_new = jnp.maximum(m, jnp.max(s, axis=0, keepdims=True))
        alpha = jnp.exp2(m - m_new)
        p = jnp.exp2(s - m_new)
        l = alpha * l + jnp.sum(p, axis=0, keepdims=True)
        acc = alpha * acc + _dot(vt_scr[ki], p.astype(BF16))
        return m_new, l, acc

    init = (jnp.full((1, 2 * blk), -jnp.inf, F32), jnp.zeros((1, 2 * blk), F32),
            jnp.zeros((HEAD_DIM, 2 * blk), F32))
    s = scores(qi)
    key = lax.broadcasted_iota(jnp.int32, s.shape, 0)
    qry = lax.broadcasted_iota(jnp.int32, s.shape, 1)
    qry = jnp.where(qry >= blk, qry - blk, qry)
    carry = update(qi, jnp.where(key <= qry, s, -jnp.inf), init)
    m, l, acc = _loop_blocks(qi, scores, update, carry, FLASH_UNROLL)

    lam = _diff_lambda(lq1_ref, lk1_ref, lq2_ref, lk2_ref, lam_init)
    on = acc / l
    o = (on[:, :blk] - lam * on[:, blk:]).T
    ms = jnp.mean(o * o, axis=-1, keepdims=True)
    o = o * lax.rsqrt(ms + EPS) * sg_ref[...] * (1.0 - lam_init)
    o_ref[...] = (o * _silu(g_ref[...])).astype(BF16)


def _sb_scores(z):
    sp = jnp.log1p(jnp.exp(-jnp.abs(z)))
    return jnp.minimum(z, 0.0) - sp, -jnp.maximum(z, 0.0) - sp


def _strict_lower(n):
    r = lax.broadcasted_iota(jnp.int32, (n, n), 0)
    c = lax.broadcasted_iota(jnp.int32, (n, n), 1)
    return jnp.where(r > c, 1.0, 0.0).astype(BF16)


def _sum_later(lk, tri):
    hi, lo = _split_hi_lo(lk)
    return _dot(hi, tri) + _dot(lo, tri)


def _sb_prompt_kernel(q_ref, k_ref, v_ref, g_ref, o_ref, *, blk):
    qi = pl.program_id(2)
    q = (q_ref[...] * (HEAD_DIM ** -0.5)).astype(BF16)
    tri = _strict_lower(blk)

    def scores(ki):
        start = pl.multiple_of(ki * blk, blk)
        return _dot_nt(q, k_ref[pl.ds(start, blk), :].astype(BF16))

    def sums(z, masked):
        lb, lk = _sb_scores(z)
        if masked:
            row = lax.broadcasted_iota(jnp.int32, lb.shape, 0)
            col = lax.broadcasted_iota(jnp.int32, lb.shape, 1)
            lk = jnp.where(col < row, lk, 0.0)
        return lb, _sum_later(lk, tri), jnp.sum(lk, axis=-1, keepdims=True)

    def accumulate(ki, parts, carry, masked):
        lb, later, total = parts
        r, acc = carry
        a = jnp.exp(lb + (r + later))
        if masked:
            row = lax.broadcasted_iota(jnp.int32, lb.shape, 0)
            col = lax.broadcasted_iota(jnp.int32, lb.shape, 1)
            a = jnp.where(col < row, a, 0.0)
        start = pl.multiple_of(ki * blk, blk)
        acc = acc + _dot(a.astype(BF16), v_ref[pl.ds(start, blk), :].astype(BF16))
        return r + total, acc

    def step(ki, carry, masked):
        return accumulate(ki, sums(scores(ki), masked), carry, masked)

    def first_two(carry):
        z_d, z_p = scores(qi), scores(qi - 1)
        parts_d, parts_p = sums(z_d, True), sums(z_p, False)
        return accumulate(qi - 1, parts_p, accumulate(qi, parts_d, carry, True), False)

    zero = (jnp.zeros((blk, 1), F32), jnp.zeros((blk, HEAD_DIM), F32))
    r, acc = lax.cond(qi >= 1, first_two, lambda c: step(qi, c, True), zero)

    def body(c):
        ki, _, r, acc = c
        r, acc = step(ki, (r, acc), False)
        return ki - 1, jnp.max(r), r, acc

    def live(c):
        return jnp.logical_and(c[0] >= 0, c[1] > SB_EXIT_LOG)

    _, _, _, acc = lax.while_loop(live, body, (qi - 2, jnp.max(r), r, acc))
    o_ref[...] = (acc * _silu(g_ref[...])).astype(BF16)


def _prompt_attention(kern, proj3, k_stack, v_stack, extra, scratch, *, layer, blk, col_q, col_g,
                      name):
    b, t, _ = proj3.shape
    n_heads = k_stack.shape[2]
    qspec = lambda c0: pl.BlockSpec((None, blk, HEAD_DIM), lambda i, h, q: (i, q, c0 + h))
    kvspec = pl.BlockSpec((None, None, None, t, HEAD_DIM), lambda i, h, q: (layer, i, h, 0, 0))
    small = [pl.BlockSpec(e.shape, lambda i, h, q: (0, 0)) for e in extra]
    return pl.pallas_call(
        kern,
        grid=(b, n_heads, t // blk),
        in_specs=[qspec(col_q), kvspec, kvspec, qspec(col_g)] + small,
        out_specs=pl.BlockSpec((None, blk, HEAD_DIM), lambda i, h, q: (i, q, h)),
        out_shape=jax.ShapeDtypeStruct((b, t, n_heads * HEAD_DIM), BF16),
        scratch_shapes=scratch,
        compiler_params=_params(("parallel", "parallel", "arbitrary")),
        name=name,
    )(proj3, k_stack, v_stack, proj3, *extra)


def _page_scores(wq_scr, k_refs, n_heads):
    return jnp.concatenate([
        jnp.concatenate([_dot_nt(wq_scr[h], k[h].astype(BF16)) for k in k_refs], axis=-1)
        for h in range(n_heads)], axis=0)


def _page_values(p, v_refs, n_heads):
    out = []
    for h in range(n_heads):
        ph = p[h * SUBLANES:(h + 1) * SUBLANES].astype(BF16)
        pv = _dot(ph[:, :LANES], v_refs[0][h].astype(BF16))
        for j in range(1, len(v_refs)):
            pv = pv + _dot(ph[:, j * LANES:(j + 1) * LANES], v_refs[j][h].astype(BF16))
        out.append(pv)
    return jnp.concatenate(out, axis=0)


def _diff_decode_kernel(pt_ref, p_ref, kv_ref, lq1_ref, lk1_ref, lq2_ref, lk2_ref, sg_ref, qg_ref,
                        pm_ref, *rest, n_pp, col_q, col_g, n_heads, lam_init):
    del pt_ref
    k_refs, v_refs = rest[:n_pp], rest[n_pp:2 * n_pp]
    o_ref, wq_scr, m_scr, l_scr, acc_scr = rest[2 * n_pp:]
    width = n_heads * HEAD_DIM
    s_idx = pl.program_id(1)

    @pl.when(s_idx == 0)
    def _():
        row = lax.broadcasted_iota(jnp.int32, (SUBLANES, HEAD_DIM), 0)
        lane = lax.broadcasted_iota(jnp.int32, (SUBLANES, HEAD_DIM), 1)
        keep = jnp.logical_or(jnp.logical_and(row == 0, lane < DIFF_HALF),
                              jnp.logical_and(row == 1, lane >= DIFF_HALF))
        for h in range(n_heads):
            qh = p_ref[:, col_q + h * HEAD_DIM:col_q + (h + 1) * HEAD_DIM]
            qh = _half_rms(jnp.broadcast_to(qh, (SUBLANES, HEAD_DIM)), pm_ref[...], qg_ref[...])
            wq_scr[h] = jnp.where(keep, qh * (DIFF_HALF ** -0.5), 0.0).astype(BF16)
        m_scr[...] = jnp.full(m_scr.shape, -jnp.inf, F32)
        l_scr[...] = jnp.zeros(l_scr.shape, F32)
        acc_scr[...] = jnp.zeros(acc_scr.shape, F32)

    s = _page_scores(wq_scr, k_refs, n_heads)
    m_old = m_scr[...]
    m_new = jnp.maximum(m_old, jnp.max(s, axis=-1, keepdims=True))
    alpha = jnp.exp(m_old - m_new)
    p = jnp.exp(s - m_new)
    l_scr[...] = alpha * l_scr[...] + jnp.sum(p, axis=-1, keepdims=True)
    acc_scr[...] = alpha * acc_scr[...] + _page_values(p, v_refs, n_heads)
    m_scr[...] = m_new

    @pl.when(s_idx == pl.num_programs(1) - 1)
    def _():
        lam = _diff_lambda(lq1_ref, lk1_ref, lq2_ref, lk2_ref, lam_init)
        outs = []
        for h in range(n_heads):
            rows = slice(h * SUBLANES, (h + 1) * SUBLANES)
            k_new = kv_ref[:, h * HEAD_DIM:(h + 1) * HEAD_DIM].astype(BF16).astype(F32)
            v_new = kv_ref[:, width + h * HEAD_DIM:width + (h + 1) * HEAD_DIM].astype(BF16).astype(F32)
            s_new = jnp.sum(wq_scr[h].astype(F32) * k_new, axis=-1, keepdims=True)
            m_old = m_scr[rows]
            m_fin = jnp.maximum(m_old, s_new)
            a_fin = jnp.exp(m_old - m_fin)
            p_new = jnp.exp(s_new - m_fin)
            l_fin = a_fin * l_scr[rows] + p_new
            on = (a_fin * acc_scr[rows] + p_new.astype(BF16).astype(F32) * v_new) / l_fin
            o = on[0:1] - lam * on[1:2]
            ms = jnp.mean(o * o, axis=-1, keepdims=True)
            o = o * lax.rsqrt(ms + EPS) * sg_ref[...] * (1.0 - lam_init)
            g = p_ref[:, col_g + h * HEAD_DIM:col_g + (h + 1) * HEAD_DIM]
            outs.append(o * _silu(g))
        o_ref[...] = jnp.concatenate(outs, axis=-1).astype(BF16)


def _sb_decode_kernel(pt_ref, p_ref, *rest, n_pp, col_q, col_g, n_heads):
    del pt_ref
    k_refs, v_refs = rest[:n_pp], rest[n_pp:2 * n_pp]
    o_ref, wq_scr, r_scr, acc_scr = rest[2 * n_pp:]
    s_idx = pl.program_id(1)

    @pl.when(s_idx == 0)
    def _():
        for h in range(n_heads):
            qh = p_ref[:, col_q + h * HEAD_DIM:col_q + (h + 1) * HEAD_DIM] * (HEAD_DIM ** -0.5)
            wq_scr[h] = jnp.broadcast_to(qh, (SUBLANES, HEAD_DIM)).astype(BF16)
        r_scr[...] = jnp.zeros(r_scr.shape, F32)
        acc_scr[...] = jnp.zeros(acc_scr.shape, F32)

    @pl.when(jnp.max(r_scr[...]) > SB_EXIT_LOG)
    def _():
        lb, lk = _sb_scores(_page_scores(wq_scr, k_refs, n_heads))
        hi, lo = _split_hi_lo(lk)
        r_i = lax.broadcasted_iota(jnp.int32, (LANES, 2 * LANES), 0)
        c_i = lax.broadcasted_iota(jnp.int32, (LANES, 2 * LANES), 1)
        tri_ones = jnp.where(jnp.logical_or(c_i >= LANES, r_i > c_i), 1.0, 0.0).astype(BF16)
        r = r_scr[...]
        weights = []
        for j in range(n_pp):
            sl = slice(j * LANES, (j + 1) * LANES)
            sums = _dot(hi[:, sl], tri_ones) + _dot(lo[:, sl], tri_ones)
            weights.append(jnp.exp(lb[:, sl] + (r + sums[:, :LANES])))
            r = r + sums[:, LANES:]
        r_scr[...] = r
        acc_scr[...] = acc_scr[...] + _page_values(jnp.concatenate(weights, axis=-1), v_refs, n_heads)

    @pl.when(s_idx == pl.num_programs(1) - 1)
    def _():
        outs = []
        for h in range(n_heads):
            g = p_ref[:, col_g + h * HEAD_DIM:col_g + (h + 1) * HEAD_DIM]
            outs.append(acc_scr[h * SUBLANES:h * SUBLANES + 1] * _silu(g))
        o_ref[...] = jnp.concatenate(outs, axis=-1).astype(BF16)


def _decode_attention(kern, page_table, rows3, cache_k, cache_v, extra, scratch, *, layer,
                      n_pp, reverse, name):
    b = rows3[0].shape[0]
    n_pages = page_table.shape[1]
    _, _, n_heads, page, _ = cache_k.shape
    n_steps = n_pages // n_pp

    def page_spec(j):
        def imap(i, s, pt):
            slot = s * n_pp + j
            if reverse:
                slot = n_pages - 1 - slot
            return (layer, pt[i, slot], 0, 0, 0)
        return pl.BlockSpec((None, None, n_heads, page, HEAD_DIM), imap)

    per_seq = [pl.BlockSpec((None, 1, r.shape[2]), lambda i, s, pt: (i, 0, 0)) for r in rows3]
    small = [pl.BlockSpec(e.shape, lambda i, s, pt: (0, 0)) for e in extra]
    pages = [page_spec(j) for j in range(n_pp)]
    grid_spec = pltpu.PrefetchScalarGridSpec(
        num_scalar_prefetch=1,
        grid=(b, n_steps),
        in_specs=per_seq + small + pages + pages,
        out_specs=pl.BlockSpec((None, 1, n_heads * HEAD_DIM), lambda i, s, pt: (i, 0, 0)),
        scratch_shapes=scratch,
    )
    return pl.pallas_call(
        kern,
        grid_spec=grid_spec,
        out_shape=jax.ShapeDtypeStruct((b, 1, n_heads * HEAD_DIM), BF16),
        compiler_params=_params(("parallel", "arbitrary")),
        name=name,
    )(page_table, *rows3, *extra, *([cache_k] * n_pp), *([cache_v] * n_pp))


def _outproj_kernel(y_ref, ga_ref, od_ref, os_ref, x_ref, wg_ref, wa_ref, wd_ref, ws_ref, o_ref):
    y = y_ref[...]
    gate = 1.0 / (1.0 + jnp.exp(-_dot(y.astype(BF16), wg_ref[...])))
    oa = (y * gate * _silu(ga_ref[...])).astype(BF16)
    acc = _dot(oa, wa_ref[...]) + _dot(od_ref[...], wd_ref[...]) + _dot(os_ref[...], ws_ref[...])
    o_ref[...] = x_ref[...] + acc


def _outproj(y2, proj, o_d, o_s, x2, w_glu, w_a, w_d, w_s, *, tm, ga_block):
    rows, d = x2.shape
    wa = y2.shape[1]
    wd, ws = o_d.shape[1], o_s.shape[1]
    row = lambda w: pl.BlockSpec((tm, w), lambda i: (i, 0))
    full = lambda a: pl.BlockSpec(a.shape, lambda i: (0, 0))
    return pl.pallas_call(
        _outproj_kernel,
        grid=(rows // tm,),
        in_specs=[row(wa), pl.BlockSpec((tm, wa), lambda i: (i, ga_block)), row(wd), row(ws), row(d),
                  full(w_glu), full(w_a), full(w_d), full(w_s)],
        out_specs=row(d),
        out_shape=jax.ShapeDtypeStruct((rows, d), F32),
        compiler_params=_params(("parallel",)),
        name="outproj",
    )(y2, proj, o_d, o_s, x2, w_glu, w_a, w_d, w_s)


def _largest_tile(n, cap):
    t = min(n, cap)
    while n % t:
        t //= 2
    return t


def kernel(x_prompt, x_sample, cache_k_diff, cache_v_diff, cache_k_sb, cache_v_sb, state_ssm_re, state_ssm_im, page_table, norm_g, w_in, ssm_lam_re, ssm_lam_im, ssm_log_dt, ssm_b_re, ssm_b_im, ssm_c_re, ssm_c_im, ssm_d, w_glu, diff_qn_g, diff_kn_g, diff_lq1, diff_lk1, diff_lq2, diff_lk2, diff_subln_g, w_out):
    bp, t, d = x_prompt.shape
    bs = x_sample.shape[0]
    depth = w_in.shape[0]
    dh, sh = cache_k_diff.shape[3], cache_k_sb.shape[3]
    g, n = ssm_lam_re.shape[1], ssm_lam_re.shape[2]
    w_ssm = g * SSM_P
    w_diff, w_sb = dh * HEAD_DIM, sh * HEAD_DIM
    assert w_diff == w_sb
    nch = w_ssm // LANES
    half = SSM_CHUNK * n
    widths = [w_ssm, w_ssm] + [w_diff] * 4 + [w_sb] * 4
    offs = [0]
    for wd_ in widths:
        offs.append(offs[-1] + wd_)
    (r_u, r_ga, r_qd, r_kd, r_vd, r_gd, r_qs, r_ks, r_vs, r_gs, ncol) = offs
    assert ncol == w_in.shape[2]
    c_ga, c_qd, c_gd, c_qs, c_gs = w_ssm, 2 * w_ssm, 2 * w_ssm + w_diff, 2 * w_ssm + 2 * w_diff, \
        2 * w_ssm + 2 * w_diff + w_sb
    cols_a = c_gs + w_sb

    tn = 512
    assert cols_a % tn == 0
    tm_p = _largest_tile(t, 1024)
    tmo_p = _largest_tile(bp * t, 512)
    blk = _largest_tile(t, 256)
    nseg = SUBLANES
    n_pp = _largest_tile(page_table.shape[1], 8)

    lane_half = jnp.arange(LANES) // DIFF_HALF
    pmat = (lane_half[:, None] == lane_half[None, :]).astype(BF16) * (1.0 / DIFF_HALF)

    ar_all, ai_all, bbr_all, bbi_all = _s5_prep(ssm_lam_re, ssm_lam_im, ssm_log_dt, ssm_b_re, ssm_b_im)

    ck_d, cv_d, ck_s, cv_s = [jnp.transpose(c, (0, 1, 3, 2, 4))
                              for c in (cache_k_diff, cache_v_diff, cache_k_sb, cache_v_sb)]
    stacks = [jnp.zeros((depth, bp, dh, t, HEAD_DIM), F32), jnp.zeros((depth, bp, dh, t, HEAD_DIM), F32),
              jnp.zeros((depth, bp, sh, t, HEAD_DIM), F32), jnp.zeros((depth, bp, sh, t, HEAD_DIM), F32)]

    xp = x_prompt.reshape(bp * t, d)
    xs = x_sample.reshape(bs, d)
    ssm_p, rows_s = [], []
    for l in range(depth):
        lam_init = 0.8 - 0.6 * math.exp(-0.3 * l)
        wl = w_in[l]
        w_a_in = jnp.concatenate([wl[:, r_u:r_kd], wl[:, r_gd:r_ks], wl[:, r_gs:]], axis=1).astype(BF16)
        w_b_in = jnp.concatenate([wl[:, r_kd:r_gd], wl[:, r_ks:r_gs]], axis=1).astype(BF16)
        wo = w_out[l].astype(BF16)
        w_a, w_d, w_s = wo[:w_ssm], wo[w_ssm:w_ssm + w_diff], wo[w_ssm + w_diff:]
        wg = w_glu[l].astype(BF16)
        q_gain = jnp.tile(diff_qn_g[l], 2).reshape(1, HEAD_DIM)
        k_gain = jnp.tile(diff_kn_g[l], 2).reshape(1, HEAD_DIM)
        ng = norm_g[l].reshape(1, d)
        bbd, cbd, ar_l, ai_l = _s5_layouts(ar_all[l], ai_all[l], bbr_all[l], bbi_all[l],
                                           ssm_c_re[l], ssm_c_im[l])
        d_skip = ssm_d[l].reshape(1, w_ssm)
        lam_vecs = [v[l].reshape(1, DIFF_HALF) for v in (diff_lq1, diff_lk1, diff_lq2, diff_lk2)]
        diff_extra = lam_vecs + [diff_subln_g[l].reshape(1, HEAD_DIM), q_gain, pmat]

        proj = _proj(xp, ng, w_a_in, tm=tm_p, tn=tn)
        stacks = _kvproj_prompt(xp, ng, w_b_in, k_gain, pmat, stacks, layer=l, seq=t, tm=tm_p)
        proj3 = proj.reshape(bp, t, cols_a)
        u = proj3[:, :, :w_ssm].reshape(bp, nseg, t // nseg, w_ssm)
        u_perm = jnp.swapaxes(u, 1, 2).reshape(bp, t, w_ssm)
        y2p, ht = _s5_prompt(u_perm, bbd, cbd, ar_l, ai_l, d_skip, nseg=nseg)
        y2 = jnp.swapaxes(y2p.reshape(bp, t // nseg, nseg, w_ssm), 1, 2).reshape(bp * t, w_ssm)
        o_d = _prompt_attention(
            functools.partial(_diff_prompt_kernel, blk=blk, lam_init=lam_init), proj3,
            stacks[0], stacks[1], diff_extra, [pltpu.VMEM((t // blk, HEAD_DIM, blk), BF16)],
            layer=l, blk=blk,
            col_q=c_qd // HEAD_DIM, col_g=c_gd // HEAD_DIM, name="diff_prompt")
        o_s = _prompt_attention(
            functools.partial(_sb_prompt_kernel, blk=blk), proj3, stacks[2], stacks[3], [], [],
            layer=l, blk=blk, col_q=c_qs // HEAD_DIM, col_g=c_gs // HEAD_DIM, name="sb_prompt")
        xp = _outproj(y2, proj, o_d.reshape(bp * t, w_diff), o_s.reshape(bp * t, w_sb), xp,
                      wg, w_a, w_d, w_s, tm=tmo_p, ga_block=c_ga // w_ssm)
        ht = ht.reshape(bp, nch, 2, SSM_CHUNK, n)
        ssm_p.append((ht[:, :, 0].reshape(bp, g, n), ht[:, :, 1].reshape(bp, g, n)))

        proj_s = _proj(xs, ng, w_a_in, tm=bs, tn=tn)
        kv_s = _kvproj_step(xs, ng, w_b_in, k_gain, pmat)
        proj_s3 = proj_s.reshape(bs, 1, cols_a)
        kv_s3 = kv_s.reshape(bs, 1, 4 * w_diff)
        h0 = jnp.concatenate([state_ssm_re[l].reshape(bs, nch, half),
                              state_ssm_im[l].reshape(bs, nch, half)], axis=-1)
        y2s, hts = _s5_step(proj_s, jnp.swapaxes(h0, 0, 1), bbd, cbd, ar_l, ai_l, d_skip)
        od_s = _decode_attention(
            functools.partial(_diff_decode_kernel, n_pp=n_pp, col_q=c_qd, col_g=c_gd, n_heads=dh,
                              lam_init=lam_init),
            page_table, [proj_s3, kv_s3[:, :, :2 * w_diff]], ck_d, cv_d, diff_extra,
            [pltpu.VMEM((dh, SUBLANES, HEAD_DIM), BF16), pltpu.VMEM((dh * SUBLANES, 1), F32),
             pltpu.VMEM((dh * SUBLANES, 1), F32), pltpu.VMEM((dh * SUBLANES, HEAD_DIM), F32)],
            layer=l, n_pp=n_pp, reverse=False, name="diff_decode")
        os_s = _decode_attention(
            functools.partial(_sb_decode_kernel, n_pp=n_pp, col_q=c_qs, col_g=c_gs, n_heads=sh),
            page_table, [proj_s3], ck_s, cv_s, [],
            [pltpu.VMEM((sh, SUBLANES, HEAD_DIM), BF16), pltpu.VMEM((sh * SUBLANES, LANES), F32),
             pltpu.VMEM((sh * SUBLANES, HEAD_DIM), F32)],
            layer=l, n_pp=n_pp, reverse=True, name="sb_decode")
        xs = _outproj(y2s, proj_s, od_s.reshape(bs, w_diff), os_s.reshape(bs, w_sb), xs,
                      wg, w_a, w_d, w_s, tm=bs, ga_block=c_ga // w_ssm)
        hts = jnp.swapaxes(hts, 0, 1).reshape(bs, nch, 2, SSM_CHUNK, n)
        kv4 = kv_s.reshape(bs, 1, 4, dh, HEAD_DIM)
        rows_s.append((kv4[:, :, 0], kv4[:, :, 1], kv4[:, :, 2], kv4[:, :, 3],
                       hts[:, :, 0].reshape(bs, g, n), hts[:, :, 1].reshape(bs, g, n)))

    kv_p = [jnp.transpose(s, (0, 1, 3, 2, 4)) for s in stacks]
    ssm_p = [jnp.stack(r) for r in zip(*ssm_p)]
    outs_s = [jnp.stack(r) for r in zip(*rows_s)]
    return (xp.reshape(bp, t, d), xs.reshape(bs, 1, d), *kv_p, *ssm_p, *outs_s)
```

```python
import functools
import math

import jax
import jax.numpy as jnp
from jax import lax
from jax.experimental import pallas as pl
from jax.experimental.pallas import tpu as pltpu

F32 = jnp.float32
BF16 = jnp.bfloat16

EPS = 1e-6
LANES = 128
SUBLANES = 8
HEAD_DIM = 128
DIFF_HALF = HEAD_DIM // 2
SSM_P = 16
SSM_N = 64
SSM_CHUNK = LANES // SSM_P
SB_EXIT_LOG = -110.0
FLASH_UNROLL = 8
LOG2_E = 1.4426950408889634
VMEM_LIMIT = 56 * 1024 * 1024


def _params(sem):
    return pltpu.CompilerParams(dimension_semantics=sem, vmem_limit_bytes=VMEM_LIMIT)


def _split_hi_lo(t):
    hi = t.astype(BF16)
    lo = (t - hi.astype(F32)).astype(BF16)
    return hi, lo


def _dot(a, b):
    return jnp.dot(a, b, preferred_element_type=F32)


def _dot_nt(a, b):
    return lax.dot_general(a, b, (((1,), (1,)), ((), ())), preferred_element_type=F32)


def _silu(x):
    return x * (1.0 / (1.0 + jnp.exp(-x)))


def _half_rms(y, pm, gain):
    hi, lo = _split_hi_lo(y * y)
    ms = _dot(hi, pm) + _dot(lo, pm)
    return y * lax.rsqrt(ms + EPS) * gain


def _normed_rows(x_ref, g_ref, xn_ref):
    @pl.when(pl.program_id(1) == 0)
    def _():
        x = x_ref[...]
        ms = jnp.mean(x * x, axis=-1, keepdims=True)
        xn_ref[...] = (x * lax.rsqrt(ms + EPS) * g_ref[...]).astype(BF16)


def _proj_kernel(x_ref, g_ref, w_ref, o_ref, xn_ref):
    _normed_rows(x_ref, g_ref, xn_ref)
    o_ref[...] = _dot(xn_ref[...], w_ref[...])


def _proj(x2, norm_g, w_bf, *, tm, tn):
    rows, d = x2.shape
    ncol = w_bf.shape[1]
    return pl.pallas_call(
        _proj_kernel,
        grid=(rows // tm, ncol // tn),
        in_specs=[
            pl.BlockSpec((tm, d), lambda i, j: (i, 0)),
            pl.BlockSpec((1, d), lambda i, j: (0, 0)),
            pl.BlockSpec((d, tn), lambda i, j: (0, j)),
        ],
        out_specs=pl.BlockSpec((tm, tn), lambda i, j: (i, j)),
        out_shape=jax.ShapeDtypeStruct((rows, ncol), F32),
        scratch_shapes=[pltpu.VMEM((tm, d), BF16)],
        compiler_params=_params(("parallel", "arbitrary")),
        name="proj",
    )(x2, norm_g, w_bf)


def _kv_heads(y, kg_ref, pm_ref, normed):
    heads = [y[:, h * HEAD_DIM:(h + 1) * HEAD_DIM] for h in range(y.shape[1] // HEAD_DIM)]
    if normed:
        heads = [_half_rms(yh, pm_ref[...], kg_ref[...]) for yh in heads]
    return heads


def _kvproj_prompt_kernel(x_ref, g_ref, w_ref, kg_ref, pm_ref, a0, a1, a2, a3,
                          kd_ref, vd_ref, ks_ref, vs_ref, xn_ref):
    del a0, a1, a2, a3
    _normed_rows(x_ref, g_ref, xn_ref)
    j = pl.program_id(1)
    y = _dot(xn_ref[...], w_ref[...])
    for idx, ref in enumerate((kd_ref, vd_ref, ks_ref, vs_ref)):
        @pl.when(j == idx)
        def _(ref=ref, idx=idx):
            for h, yh in enumerate(_kv_heads(y, kg_ref, pm_ref, idx == 0)):
                ref[h] = yh


def _kvproj_prompt(x2, norm_g, w_bf, k_gain, pmat, stacks, *, layer, seq, tm):
    rows, d = x2.shape
    width = w_bf.shape[1] // 4
    n_heads = width // HEAD_DIM
    tpb = seq // tm
    any_spec = pl.BlockSpec(memory_space=pl.ANY)
    out_spec = pl.BlockSpec((None, None, n_heads, tm, HEAD_DIM),
                            lambda i, j: (layer, i // tpb, 0, i % tpb, 0))
    return pl.pallas_call(
        _kvproj_prompt_kernel,
        grid=(rows // tm, 4),
        in_specs=[
            pl.BlockSpec((tm, d), lambda i, j: (i, 0)),
            pl.BlockSpec((1, d), lambda i, j: (0, 0)),
            pl.BlockSpec((d, width), lambda i, j: (0, j)),
            pl.BlockSpec((1, HEAD_DIM), lambda i, j: (0, 0)),
            pl.BlockSpec((LANES, LANES), lambda i, j: (0, 0)),
            any_spec, any_spec, any_spec, any_spec,
        ],
        out_specs=[out_spec] * 4,
        out_shape=[jax.ShapeDtypeStruct(s.shape, s.dtype) for s in stacks],
        input_output_aliases={5: 0, 6: 1, 7: 2, 8: 3},
        scratch_shapes=[pltpu.VMEM((tm, d), BF16)],
        compiler_params=_params(("parallel", "arbitrary")),
        name="kvproj_prompt",
    )(x2, norm_g, w_bf, k_gain, pmat, *stacks)


def _kvproj_step_kernel(x_ref, g_ref, w_ref, kg_ref, pm_ref, o_ref, xn_ref):
    _normed_rows(x_ref, g_ref, xn_ref)
    j = pl.program_id(1)
    y = _dot(xn_ref[...], w_ref[...])
    for normed in (True, False):
        @pl.when((j == 0) == normed)
        def _(normed=normed):
            for h, yh in enumerate(_kv_heads(y, kg_ref, pm_ref, normed)):
                o_ref[:, h * HEAD_DIM:(h + 1) * HEAD_DIM] = yh


def _kvproj_step(x2, norm_g, w_bf, k_gain, pmat):
    rows, d = x2.shape
    ncol = w_bf.shape[1]
    width = ncol // 4
    return pl.pallas_call(
        _kvproj_step_kernel,
        grid=(1, 4),
        in_specs=[
            pl.BlockSpec((rows, d), lambda i, j: (i, 0)),
            pl.BlockSpec((1, d), lambda i, j: (0, 0)),
            pl.BlockSpec((d, width), lambda i, j: (0, j)),
            pl.BlockSpec((1, HEAD_DIM), lambda i, j: (0, 0)),
            pl.BlockSpec((LANES, LANES), lambda i, j: (0, 0)),
        ],
        out_specs=pl.BlockSpec((rows, width), lambda i, j: (i, j)),
        out_shape=jax.ShapeDtypeStruct((rows, ncol), F32),
        scratch_shapes=[pltpu.VMEM((rows, d), BF16)],
        compiler_params=_params(("parallel", "arbitrary")),
        name="kvproj_step",
    )(x2, norm_g, w_bf, k_gain, pmat)


def _s5_prep_kernel(lr_ref, li_ref, ldt_ref, br_ref, bi_ref, ar_ref, ai_ref, bbr_ref, bbi_ref):
    lr, li = lr_ref[...], li_ref[...]
    dt = jnp.exp(ldt_ref[...])
    mag = jnp.exp(lr * dt)
    ar, ai = mag * jnp.cos(li * dt), mag * jnp.sin(li * dt)
    den = lr * lr + li * li
    kr = ((ar - 1.0) * lr + ai * li) / den
    ki = (ai * lr - (ar - 1.0) * li) / den
    br, bi = br_ref[...], bi_ref[...]
    ar_ref[...] = ar
    ai_ref[...] = ai
    bbr_ref[...] = kr * br - ki * bi
    bbi_ref[...] = kr * bi + ki * br


def _s5_prep(lam_re, lam_im, log_dt, b_re, b_im):
    dep, g, n = lam_re.shape
    p = b_re.shape[-1]
    rows = dep * g * p

    def rep(a):
        return jnp.broadcast_to(a[:, :, None, :], (dep, g, p, n)).reshape(rows, n)

    ldt = jnp.broadcast_to(log_dt[:, :, None, None], (dep, g, p, n)).reshape(rows, n)
    brt = jnp.swapaxes(b_re, 2, 3).reshape(rows, n)
    bit = jnp.swapaxes(b_im, 2, 3).reshape(rows, n)
    shp = jax.ShapeDtypeStruct((rows, n), F32)
    ar, ai, bbr, bbi = pl.pallas_call(
        _s5_prep_kernel, out_shape=(shp, shp, shp, shp), name="s5_prep",
    )(rep(lam_re), rep(lam_im), ldt, brt, bit)
    ar = ar.reshape(dep, g, p, n)[:, :, 0]
    ai = ai.reshape(dep, g, p, n)[:, :, 0]
    return ar, ai, bbr.reshape(dep, g, p, n), bbi.reshape(dep, g, p, n)


def _s5_layouts(ar, ai, bbr, bbi, c_re, c_im):
    g, n = ar.shape
    nch = g // SSM_CHUNK
    half = SSM_CHUNK * n
    eye = jnp.eye(SSM_CHUNK, dtype=F32)

    def in_bd(bb):
        bb = bb.reshape(nch, SSM_CHUNK, SSM_P, n)
        return (bb[:, :, :, None, :] * eye[None, :, None, :, None]).reshape(nch, LANES, half)

    def out_bd(c):
        c = jnp.swapaxes(c.reshape(nch, SSM_CHUNK, SSM_P, n), 2, 3)
        return (c[:, :, :, None, :] * eye[None, :, None, :, None]).reshape(nch, half, LANES)

    bbd = jnp.concatenate([in_bd(bbr), in_bd(bbi)], axis=-1).astype(BF16)
    cbd = jnp.concatenate([out_bd(c_re), -out_bd(c_im)], axis=1).astype(BF16)
    return bbd, cbd, ar.reshape(nch, 1, half), ai.reshape(nch, 1, half)


def _cmul(ar, ai, br, bi):
    return ar * br - ai * bi, ar * bi + ai * br


def _s5_prompt_kernel(u_ref, bbd_ref, cbd_ref, ar_ref, ai_ref, d_ref, y_ref, ht_ref, x_scr,
                      *, nseg, seglen, unroll):
    half = ar_ref.shape[-1]
    jc = min(seglen, 64)
    rc = jc * nseg

    def fill(c, carry):
        r0 = pl.multiple_of(c * rc, rc)
        x = _dot(u_ref[pl.ds(r0, rc), :].astype(BF16), bbd_ref[...])
        x_scr[pl.ds(c * jc, jc)] = x.reshape(jc, nseg, 2 * half)
        return carry

    lax.fori_loop(0, seglen // jc, fill, 0)
    ar = jnp.broadcast_to(ar_ref[...], (nseg, half))
    ai = jnp.broadcast_to(ai_ref[...], (nseg, half))

    def scan_step(j, carry):
        hr, hi = carry
        pr, pi = _cmul(ar, ai, hr, hi)
        hr = pr + x_scr[j, :, :half]
        hi = pi + x_scr[j, :, half:]
        x_scr[j, :, :half] = hr
        x_scr[j, :, half:] = hi
        return hr, hi

    zero = jnp.zeros((nseg, half), F32)
    er, ei = lax.fori_loop(0, seglen, scan_step, (zero, zero), unroll=unroll)

    sr, si = ar_ref[...], ai_ref[...]
    for _ in range(int(math.log2(seglen))):
        sr, si = _cmul(sr, si, sr, si)

    cin_r, cin_i = [jnp.zeros((1, half), F32)], [jnp.zeros((1, half), F32)]
    for s in range(nseg):
        pr, pi = _cmul(sr, si, cin_r[-1], cin_i[-1])
        cin_r.append(er[s:s + 1] + pr)
        cin_i.append(ei[s:s + 1] + pi)
    ht_ref[:, :half] = cin_r[-1]
    ht_ref[:, half:] = cin_i[-1]
    cr = jnp.concatenate(cin_r[:-1], axis=0)
    ci = jnp.concatenate(cin_i[:-1], axis=0)

    def fix_step(j, carry):
        pr, pi = carry
        dr, di = _cmul(pr, pi, cr, ci)
        x_scr[j, :, :half] = x_scr[j, :, :half] + dr
        x_scr[j, :, half:] = x_scr[j, :, half:] + di
        return _cmul(pr, pi, ar, ai)

    lax.fori_loop(0, seglen, fix_step, (ar, ai), unroll=unroll)

    def drain(c, carry):
        r0 = pl.multiple_of(c * rc, rc)
        h = x_scr[pl.ds(c * jc, jc)].reshape(rc, 2 * half)
        y = _dot(h.astype(BF16), cbd_ref[...]) + d_ref[...] * u_ref[pl.ds(r0, rc), :]
        y_ref[pl.ds(r0, rc), :] = jax.nn.gelu(y)
        return carry

    lax.fori_loop(0, seglen // jc, drain, 0)


def _s5_prompt(u_perm, bbd, cbd, ar_l, ai_l, d_skip, *, nseg):
    b, t, w = u_perm.shape
    nch, _, cols = bbd.shape
    half = cols // 2
    seglen = t // nseg
    kern = functools.partial(_s5_prompt_kernel, nseg=nseg, seglen=seglen, unroll=min(8, seglen))
    return pl.pallas_call(
        kern,
        grid=(b, nch),
        in_specs=[
            pl.BlockSpec((None, t, LANES), lambda i, c: (i, 0, c)),
            pl.BlockSpec((None, LANES, cols), lambda i, c: (c, 0, 0)),
            pl.BlockSpec((None, cols, LANES), lambda i, c: (c, 0, 0)),
            pl.BlockSpec((None, 1, half), lambda i, c: (c, 0, 0)),
            pl.BlockSpec((None, 1, half), lambda i, c: (c, 0, 0)),
            pl.BlockSpec((1, LANES), lambda i, c: (0, c)),
        ],
        out_specs=[
            pl.BlockSpec((None, t, LANES), lambda i, c: (i, 0, c)),
            pl.BlockSpec((None, None, 1, cols), lambda i, c: (i, c, 0, 0)),
        ],
        out_shape=[jax.ShapeDtypeStruct((b, t, w), F32),
                   jax.ShapeDtypeStruct((b, nch, 1, cols), F32)],
        scratch_shapes=[pltpu.VMEM((seglen, nseg, cols), F32)],
        compiler_params=_params(("parallel", "arbitrary")),
        name="s5_prompt",
    )(u_perm, bbd, cbd, ar_l, ai_l, d_skip)


def _s5_step_kernel(p_ref, h0_ref, bbd_ref, cbd_ref, ar_ref, ai_ref, d_ref, y_ref, ht_ref, *, nch):
    half = ar_ref.shape[-1]
    for c in range(nch):
        u = p_ref[:, c * LANES:(c + 1) * LANES]
        x = _dot(u.astype(BF16), bbd_ref[c])
        h0r, h0i = h0_ref[c, :, :half], h0_ref[c, :, half:]
        pr, pi = _cmul(ar_ref[c], ai_ref[c], h0r, h0i)
        hr = pr + x[:, :half]
        hi = pi + x[:, half:]
        ht_ref[c, :, :half] = hr
        ht_ref[c, :, half:] = hi
        h = jnp.concatenate([hr, hi], axis=-1)
        y = _dot(h.astype(BF16), cbd_ref[c]) + d_ref[:, c * LANES:(c + 1) * LANES] * u
        y_ref[:, c * LANES:(c + 1) * LANES] = jax.nn.gelu(y)


def _s5_step(proj_s, h0_l, bbd, cbd, ar_l, ai_l, d_skip):
    b = proj_s.shape[0]
    nch, _, cols = bbd.shape
    w = nch * LANES
    kern = functools.partial(_s5_step_kernel, nch=nch)
    return pl.pallas_call(
        kern,
        out_shape=[jax.ShapeDtypeStruct((b, w), F32), jax.ShapeDtypeStruct((nch, b, cols), F32)],
        compiler_params=pltpu.CompilerParams(vmem_limit_bytes=VMEM_LIMIT),
        name="s5_step",
    )(proj_s, h0_l, bbd, cbd, ar_l, ai_l, d_skip)


def _run_blocks(ks, scores, update, carry):
    s_next = scores(ks[0])
    for idx, k in enumerate(ks):
        s_cur = s_next
        if idx + 1 < len(ks):
            s_next = scores(ks[idx + 1])
        carry = update(k, s_cur, carry)
    return carry


def _loop_blocks(n, scores, update, carry, unroll):
    carry = lax.fori_loop(
        0, n // unroll,
        lambda i, c: _run_blocks([unroll * i + u for u in range(unroll)], scores, update, c), carry)
    base = (n // unroll) * unroll
    rem = n - base
    width = unroll // 2
    while width >= 1:
        take = (rem & width) != 0
        carry = lax.cond(
            take,
            lambda c, base=base, width=width: _run_blocks([base + u for u in range(width)],
                                                          scores, update, c),
            lambda c: c, carry)
        base = base + jnp.where(take, width, 0)
        width //= 2
    return carry


def _diff_lambda(lq1_ref, lk1_ref, lq2_ref, lk2_ref, lam_init):
    s1 = jnp.sum(lq1_ref[...] * lk1_ref[...], axis=-1, keepdims=True)
    s2 = jnp.sum(lq2_ref[...] * lk2_ref[...], axis=-1, keepdims=True)
    return jnp.exp(s1) - jnp.exp(s2) + lam_init


def _diff_prompt_kernel(q_ref, k_ref, v_ref, g_ref, lq1_ref, lk1_ref, lq2_ref, lk2_ref, sg_ref,
                        qg_ref, pm_ref, o_ref, vt_scr, *, blk, lam_init):
    qi = pl.program_id(2)

    @pl.when(qi == 0)
    def _():
        for c in range(vt_scr.shape[0]):
            vt_scr[c] = v_ref[c * blk:(c + 1) * blk, :].T.astype(BF16)

    q = _half_rms(q_ref[...], pm_ref[...], qg_ref[...]) * (DIFF_HALF ** -0.5 * LOG2_E)
    lane = lax.broadcasted_iota(jnp.int32, q.shape, 1)
    q2 = jnp.concatenate([jnp.where(lane < DIFF_HALF, q, 0.0),
                          jnp.where(lane >= DIFF_HALF, q, 0.0)], axis=0).astype(BF16)

    def scores(ki):
        start = pl.multiple_of(ki * blk, blk)
        return _dot_nt(k_ref[pl.ds(start, blk), :].astype(BF16), q2)

    def update(ki, s, carry):
        m, l, acc = carry
        m_new = jnp.maximum(m, jnp.max(s, axis=0, keepdims=True))
        alpha = jnp.exp2(m - m_new)
        p = jnp.exp2(s - m_new)
        l = alpha * l + jnp.sum(p, axis=0, keepdims=True)
        acc = alpha * acc + _dot(vt_scr[ki], p.astype(BF16))
        return m_new, l, acc

    init = (jnp.full((1, 2 * blk), -jnp.inf, F32), jnp.zeros((1, 2 * blk), F32),
            jnp.zeros((HEAD_DIM, 2 * blk), F32))
    s = scores(qi)
    key = lax.broadcasted_iota(jnp.int32, s.shape, 0)
    qry = lax.broadcasted_iota(jnp.int32, s.shape, 1)
    qry = jnp.where(qry >= blk, qry - blk, qry)
    carry = update(qi, jnp.where(key <= qry, s, -jnp.inf), init)
    m, l, acc = _loop_blocks(qi, scores, update, carry, FLASH_UNROLL)

    lam = _diff_lambda(lq1_ref, lk1_ref, lq2_ref, lk2_ref, lam_init)
    on = acc * (1.0 / l)
    o = (on[:, :blk] - lam * on[:, blk:]).T
    ms = jnp.mean(o * o, axis=-1, keepdims=True)
    o = o * lax.rsqrt(ms + EPS) * sg_ref[...] * (1.0 - lam_init)
    o_ref[...] = (o * _silu(g_ref[...])).astype(BF16)


def _sb_scores(z):
    sp = jnp.log1p(jnp.exp(-jnp.abs(z)))
    return jnp.minimum(z, 0.0) - sp, -jnp.maximum(z, 0.0) - sp


def _strict_lower(n):
    r = lax.broadcasted_iota(jnp.int32, (n, n), 0)
    c = lax.broadcasted_iota(jnp.int32, (n, n), 1)
    return jnp.where(r > c, 1.0, 0.0).astype(BF16)


def _sum_later(lk, tri):
    hi, lo = _split_hi_lo(lk)
    return _dot(hi, tri) + _dot(lo, tri)


def _sb_prompt_kernel(q_ref, k_ref, v_ref, g_ref, o_ref, *, blk):
    qi = pl.program_id(2)
    q = (q_ref[...] * (HEAD_DIM ** -0.5)).astype(BF16)
    tri = _strict_lower(blk)

    def scores(ki):
        start = pl.multiple_of(ki * blk, blk)
        return _dot_nt(q, k_ref[pl.ds(start, blk), :].astype(BF16))

    def sums(z, masked):
        lb, lk = _sb_scores(z)
        if masked:
            row = lax.broadcasted_iota(jnp.int32, lb.shape, 0)
            col = lax.broadcasted_iota(jnp.int32, lb.shape, 1)
            lk = jnp.where(col < row, lk, 0.0)
        return lb, _sum_later(lk, tri), jnp.sum(lk, axis=-1, keepdims=True)

    def accumulate(ki, parts, carry, masked):
        lb, later, total = parts
        r, acc = carry
        a = jnp.exp(lb + (r + later))
        if masked:
            row = lax.broadcasted_iota(jnp.int32, lb.shape, 0)
            col = lax.broadcasted_iota(jnp.int32, lb.shape, 1)
            a = jnp.where(col < row, a, 0.0)
        start = pl.multiple_of(ki * blk, blk)
        acc = acc + _dot(a.astype(BF16), v_ref[pl.ds(start, blk), :].astype(BF16))
        return r + total, acc

    def step(ki, carry, masked):
        return accumulate(ki, sums(scores(ki), masked), carry, masked)

    def first_two(carry):
        z_d, z_p = scores(qi), scores(qi - 1)
        parts_d, parts_p = sums(z_d, True), sums(z_p, False)
        return accumulate(qi - 1, parts_p, accumulate(qi, parts_d, carry, True), False)

    zero = (jnp.zeros((blk, 1), F32), jnp.zeros((blk, HEAD_DIM), F32))
    r, acc = lax.cond(qi >= 1, first_two, lambda c: step(qi, c, True), zero)

    def body(c):
        ki, _, r, acc = c
        r, acc = step(ki, (r, acc), False)
        return ki - 1, jnp.max(r), r, acc

    def live(c):
        return jnp.logical_and(c[0] >= 0, c[1] > SB_EXIT_LOG)

    _, _, _, acc = lax.while_loop(live, body, (qi - 2, jnp.max(r), r, acc))
    o_ref[...] = (acc * _silu(g_ref[...])).astype(BF16)


def _prompt_attention(kern, proj3, k_stack, v_stack, extra, scratch, *, layer, blk, col_q, col_g,
                      name):
    b, t, _ = proj3.shape
    n_heads = k_stack.shape[2]
    qspec = lambda c0: pl.BlockSpec((None, blk, HEAD_DIM), lambda i, h, q: (i, q, c0 + h))
    kvspec = pl.BlockSpec((None, None, None, t, HEAD_DIM), lambda i, h, q: (layer, i, h, 0, 0))
    small = [pl.BlockSpec(e.shape, lambda i, h, q: (0, 0)) for e in extra]
    return pl.pallas_call(
        kern,
        grid=(b, n_heads, t // blk),
        in_specs=[qspec(col_q), kvspec, kvspec, qspec(col_g)] + small,
        out_specs=pl.BlockSpec((None, blk, HEAD_DIM), lambda i, h, q: (i, q, h)),
        out_shape=jax.ShapeDtypeStruct((b, t, n_heads * HEAD_DIM), BF16),
        scratch_shapes=scratch,
        compiler_params=_params(("parallel", "parallel", "arbitrary")),
        name=name,
    )(proj3, k_stack, v_stack, proj3, *extra)


def _page_scores(wq_scr, k_refs, n_heads):
    return jnp.concatenate([
        jnp.concatenate([_dot_nt(wq_scr[h], k[h].astype(BF16)) for k in k_refs], axis=-1)
        for h in range(n_heads)], axis=0)


def _page_values(p, v_refs, n_heads):
    out = []
    for h in range(n_heads):
        ph = p[h * SUBLANES:(h + 1) * SUBLANES].astype(BF16)
        pv = _dot(ph[:, :LANES], v_refs[0][h].astype(BF16))
        for j in range(1, len(v_refs)):
            pv = pv + _dot(ph[:, j * LANES:(j + 1) * LANES], v_refs[j][h].astype(BF16))
        out.append(pv)
    return jnp.concatenate(out, axis=0)


def _diff_decode_kernel(pt_ref, p_ref, kv_ref, lq1_ref, lk1_ref, lq2_ref, lk2_ref, sg_ref, qg_ref,
                        pm_ref, *rest, n_pp, col_q, col_g, n_heads, lam_init):
    del pt_ref
    k_refs, v_refs = rest[:n_pp], rest[n_pp:2 * n_pp]
    o_ref, wq_scr, m_scr, l_scr, acc_scr = rest[2 * n_pp:]
    width = n_heads * HEAD_DIM
    s_idx = pl.program_id(1)

    @pl.when(s_idx == 0)
    def _():
        row = lax.broadcasted_iota(jnp.int32, (SUBLANES, HEAD_DIM), 0)
        lane = lax.broadcasted_iota(jnp.int32, (SUBLANES, HEAD_DIM), 1)
        keep = jnp.logical_or(jnp.logical_and(row == 0, lane < DIFF_HALF),
                              jnp.logical_and(row == 1, lane >= DIFF_HALF))
        for h in range(n_heads):
            qh = p_ref[:, col_q + h * HEAD_DIM:col_q + (h + 1) * HEAD_DIM]
            qh = _half_rms(jnp.broadcast_to(qh, (SUBLANES, HEAD_DIM)), pm_ref[...], qg_ref[...])
            wq_scr[h] = jnp.where(keep, qh * (DIFF_HALF ** -0.5), 0.0).astype(BF16)
        m_scr[...] = jnp.full(m_scr.shape, -jnp.inf, F32)
        l_scr[...] = jnp.zeros(l_scr.shape, F32)
        acc_scr[...] = jnp.zeros(acc_scr.shape, F32)

    s = _page_scores(wq_scr, k_refs, n_heads)
    m_old = m_scr[...]
    m_new = jnp.maximum(m_old, jnp.max(s, axis=-1, keepdims=True))
    alpha = jnp.exp(m_old - m_new)
    p = jnp.exp(s - m_new)
    l_scr[...] = alpha * l_scr[...] + jnp.sum(p, axis=-1, keepdims=True)
    acc_scr[...] = alpha * acc_scr[...] + _page_values(p, v_refs, n_heads)
    m_scr[...] = m_new

    @pl.when(s_idx == pl.num_programs(1) - 1)
    def _():
        lam = _diff_lambda(lq1_ref, lk1_ref, lq2_ref, lk2_ref, lam_init)
        outs = []
        for h in range(n_heads):
            rows = slice(h * SUBLANES, (h + 1) * SUBLANES)
            k_new = kv_ref[:, h * HEAD_DIM:(h + 1) * HEAD_DIM].astype(BF16).astype(F32)
            v_new = kv_ref[:, width + h * HEAD_DIM:width + (h + 1) * HEAD_DIM].astype(BF16).astype(F32)
            s_new = jnp.sum(wq_scr[h].astype(F32) * k_new, axis=-1, keepdims=True)
            m_old = m_scr[rows]
            m_fin = jnp.maximum(m_old, s_new)
            a_fin = jnp.exp(m_old - m_fin)
            p_new = jnp.exp(s_new - m_fin)
            l_fin = a_fin * l_scr[rows] + p_new
            on = (a_fin * acc_scr[rows] + p_new.astype(BF16).astype(F32) * v_new) / l_fin
            o = on[0:1] - lam * on[1:2]
            ms = jnp.mean(o * o, axis=-1, keepdims=True)
            o = o * lax.rsqrt(ms + EPS) * sg_ref[...] * (1.0 - lam_init)
            g = p_ref[:, col_g + h * HEAD_DIM:col_g + (h + 1) * HEAD_DIM]
            outs.append(o * _silu(g))
        o_ref[...] = jnp.concatenate(outs, axis=-1).astype(BF16)


def _sb_decode_kernel(pt_ref, p_ref, ck_ref, cv_ref, o_ref, wq_scr, r_scr, acc_scr, kbuf, vbuf, sem,
                      *, layer, n_pp, col_q, col_g, n_heads):
    seq = pl.program_id(0)
    n_pages = pt_ref.shape[1]
    n_groups = n_pages // n_pp

    def copies(g, slot):
        out = []
        for j in range(n_pp):
            page = pt_ref[seq, n_pages - 1 - (g * n_pp + j)]
            out.append(pltpu.make_async_copy(ck_ref.at[layer, page], kbuf.at[slot, j], sem.at[slot, 0, j]))
            out.append(pltpu.make_async_copy(cv_ref.at[layer, page], vbuf.at[slot, j], sem.at[slot, 1, j]))
        return out

    for c in copies(0, 0):
        c.start()
    for h in range(n_heads):
        qh = p_ref[:, col_q + h * HEAD_DIM:col_q + (h + 1) * HEAD_DIM] * (HEAD_DIM ** -0.5)
        wq_scr[h] = jnp.broadcast_to(qh, (SUBLANES, HEAD_DIM)).astype(BF16)
    r_scr[...] = jnp.zeros(r_scr.shape, F32)
    acc_scr[...] = jnp.zeros(acc_scr.shape, F32)

    def group(carry):
        g, _ = carry
        slot = g & 1

        @pl.when(g + 1 < n_groups)
        def _():
            for c in copies(g + 1, 1 - slot):
                c.start()

        for c in copies(g, slot):
            c.wait()
        k_refs = [kbuf.at[slot, j] for j in range(n_pp)]
        v_refs = [vbuf.at[slot, j] for j in range(n_pp)]
        lb, lk = _sb_scores(_page_scores(wq_scr, k_refs, n_heads))
        hi, lo = _split_hi_lo(lk)
        r_i = lax.broadcasted_iota(jnp.int32, (LANES, 2 * LANES), 0)
        c_i = lax.broadcasted_iota(jnp.int32, (LANES, 2 * LANES), 1)
        tri_ones = jnp.where(jnp.logical_or(c_i >= LANES, r_i > c_i), 1.0, 0.0).astype(BF16)
        r = r_scr[...]
        weights = []
        for j in range(n_pp):
            sl = slice(j * LANES, (j + 1) * LANES)
            sums = _dot(hi[:, sl], tri_ones) + _dot(lo[:, sl], tri_ones)
            weights.append(jnp.exp(lb[:, sl] + (r + sums[:, :LANES])))
            r = r + sums[:, LANES:]
        r_scr[...] = r
        acc_scr[...] = acc_scr[...] + _page_values(jnp.concatenate(weights, axis=-1), v_refs, n_heads)
        return g + 1, jnp.max(r)

    def live(carry):
        return jnp.logical_and(carry[0] < n_groups, carry[1] > SB_EXIT_LOG)

    g_end, _ = lax.while_loop(live, group, (jnp.int32(0), jnp.float32(0.0)))

    @pl.when(g_end < n_groups)
    def _():
        for c in copies(g_end, g_end & 1):
            c.wait()

    outs = []
    for h in range(n_heads):
        gate = p_ref[:, col_g + h * HEAD_DIM:col_g + (h + 1) * HEAD_DIM]
        outs.append(acc_scr[h * SUBLANES:h * SUBLANES + 1] * _silu(gate))
    o_ref[...] = jnp.concatenate(outs, axis=-1).astype(BF16)


def _sb_decode(page_table, proj_s3, cache_k, cache_v, *, layer, n_pp, col_q, col_g):
    b, _, cols = proj_s3.shape
    _, _, n_heads, page, _ = cache_k.shape
    rows = n_heads * SUBLANES
    kern = functools.partial(_sb_decode_kernel, layer=layer, n_pp=n_pp, col_q=col_q, col_g=col_g,
                             n_heads=n_heads)
    buf = pltpu.VMEM((2, n_pp, n_heads, page, HEAD_DIM), F32)
    grid_spec = pltpu.PrefetchScalarGridSpec(
        num_scalar_prefetch=1,
        grid=(b,),
        in_specs=[pl.BlockSpec((None, 1, cols), lambda i, pt: (i, 0, 0)),
                  pl.BlockSpec(memory_space=pl.ANY), pl.BlockSpec(memory_space=pl.ANY)],
        out_specs=pl.BlockSpec((None, 1, n_heads * HEAD_DIM), lambda i, pt: (i, 0, 0)),
        scratch_shapes=[pltpu.VMEM((n_heads, SUBLANES, HEAD_DIM), BF16), pltpu.VMEM((rows, LANES), F32),
                        pltpu.VMEM((rows, HEAD_DIM), F32), buf, buf,
                        pltpu.SemaphoreType.DMA((2, 2, n_pp))],
    )
    return pl.pallas_call(
        kern,
        grid_spec=grid_spec,
        out_shape=jax.ShapeDtypeStruct((b, 1, n_heads * HEAD_DIM), BF16),
        compiler_params=_params(("arbitrary",)),
        name="sb_decode",
    )(page_table, proj_s3, cache_k, cache_v)


def _decode_attention(kern, page_table, rows3, cache_k, cache_v, extra, scratch, *, layer,
                      n_pp, reverse, name):
    b = rows3[0].shape[0]
    n_pages = page_table.shape[1]
    _, _, n_heads, page, _ = cache_k.shape
    n_steps = n_pages // n_pp

    def page_spec(j):
        def imap(i, s, pt):
            slot = s * n_pp + j
            if reverse:
                slot = n_pages - 1 - slot
            return (layer, pt[i, slot], 0, 0, 0)
        return pl.BlockSpec((None, None, n_heads, page, HEAD_DIM), imap)

    per_seq = [pl.BlockSpec((None, 1, r.shape[2]), lambda i, s, pt: (i, 0, 0)) for r in rows3]
    small = [pl.BlockSpec(e.shape, lambda i, s, pt: (0, 0)) for e in extra]
    pages = [page_spec(j) for j in range(n_pp)]
    grid_spec = pltpu.PrefetchScalarGridSpec(
        num_scalar_prefetch=1,
        grid=(b, n_steps),
        in_specs=per_seq + small + pages + pages,
        out_specs=pl.BlockSpec((None, 1, n_heads * HEAD_DIM), lambda i, s, pt: (i, 0, 0)),
        scratch_shapes=scratch,
    )
    return pl.pallas_call(
        kern,
        grid_spec=grid_spec,
        out_shape=jax.ShapeDtypeStruct((b, 1, n_heads * HEAD_DIM), BF16),
        compiler_params=_params(("parallel", "arbitrary")),
        name=name,
    )(page_table, *rows3, *extra, *([cache_k] * n_pp), *([cache_v] * n_pp))


def _outproj_kernel(y_ref, ga_ref, od_ref, os_ref, x_ref, wg_ref, wa_ref, wd_ref, ws_ref, o_ref):
    y = y_ref[...]
    gate = 1.0 / (1.0 + jnp.exp(-_dot(y.astype(BF16), wg_ref[...])))
    oa = (y * gate * _silu(ga_ref[...])).astype(BF16)
    acc = _dot(oa, wa_ref[...]) + _dot(od_ref[...], wd_ref[...]) + _dot(os_ref[...], ws_ref[...])
    o_ref[...] = x_ref[...] + acc


def _outproj(y2, proj, o_d, o_s, x2, w_glu, w_a, w_d, w_s, *, tm, ga_block):
    rows, d = x2.shape
    wa = y2.shape[1]
    wd, ws = o_d.shape[1], o_s.shape[1]
    row = lambda w: pl.BlockSpec((tm, w), lambda i: (i, 0))
    full = lambda a: pl.BlockSpec(a.shape, lambda i: (0, 0))
    return pl.pallas_call(
        _outproj_kernel,
        grid=(rows // tm,),
        in_specs=[row(wa), pl.BlockSpec((tm, wa), lambda i: (i, ga_block)), row(wd), row(ws), row(d),
                  full(w_glu), full(w_a), full(w_d), full(w_s)],
        out_specs=row(d),
        out_shape=jax.ShapeDtypeStruct((rows, d), F32),
        compiler_params=_params(("parallel",)),
        name="outproj",
    )(y2, proj, o_d, o_s, x2, w_glu, w_a, w_d, w_s)


def _largest_tile(n, cap):
    t = min(n, cap)
    while n % t:
        t //= 2
    return t


def kernel(x_prompt, x_sample, cache_k_diff, cache_v_diff, cache_k_sb, cache_v_sb, state_ssm_re, state_ssm_im, page_table, norm_g, w_in, ssm_lam_re, ssm_lam_im, ssm_log_dt, ssm_b_re, ssm_b_im, ssm_c_re, ssm_c_im, ssm_d, w_glu, diff_qn_g, diff_kn_g, diff_lq1, diff_lk1, diff_lq2, diff_lk2, diff_subln_g, w_out):
    bp, t, d = x_prompt.shape
    bs = x_sample.shape[0]
    depth = w_in.shape[0]
    dh, sh = cache_k_diff.shape[3], cache_k_sb.shape[3]
    g, n = ssm_lam_re.shape[1], ssm_lam_re.shape[2]
    w_ssm = g * SSM_P
    w_diff, w_sb = dh * HEAD_DIM, sh * HEAD_DIM
    assert w_diff == w_sb
    nch = w_ssm // LANES
    half = SSM_CHUNK * n
    widths = [w_ssm, w_ssm] + [w_diff] * 4 + [w_sb] * 4
    offs = [0]
    for wd_ in widths:
        offs.append(offs[-1] + wd_)
    (r_u, r_ga, r_qd, r_kd, r_vd, r_gd, r_qs, r_ks, r_vs, r_gs, ncol) = offs
    assert ncol == w_in.shape[2]
    c_ga, c_qd, c_gd, c_qs, c_gs = w_ssm, 2 * w_ssm, 2 * w_ssm + w_diff, 2 * w_ssm + 2 * w_diff, \
        2 * w_ssm + 2 * w_diff + w_sb
    cols_a = c_gs + w_sb

    tn = 512
    assert cols_a % tn == 0
    tm_p = _largest_tile(t, 1024)
    tmo_p = _largest_tile(bp * t, 512)
    blk = _largest_tile(t, 256)
    nseg = SUBLANES
    n_pp = _largest_tile(page_table.shape[1], 16)
    n_pp_sb = _largest_tile(page_table.shape[1], 2)

    lane_half = jnp.arange(LANES) // DIFF_HALF
    pmat = (lane_half[:, None] == lane_half[None, :]).astype(BF16) * (1.0 / DIFF_HALF)

    ar_all, ai_all, bbr_all, bbi_all = _s5_prep(ssm_lam_re, ssm_lam_im, ssm_log_dt, ssm_b_re, ssm_b_im)

    ck_d, cv_d, ck_s, cv_s = [jnp.transpose(c, (0, 1, 3, 2, 4))
                              for c in (cache_k_diff, cache_v_diff, cache_k_sb, cache_v_sb)]
    stacks = [jnp.zeros((depth, bp, dh, t, HEAD_DIM), F32), jnp.zeros((depth, bp, dh, t, HEAD_DIM), F32),
              jnp.zeros((depth, bp, sh, t, HEAD_DIM), F32), jnp.zeros((depth, bp, sh, t, HEAD_DIM), F32)]

    xp = x_prompt.reshape(bp * t, d)
    xs = x_sample.reshape(bs, d)
    ssm_p, rows_s = [], []
    for l in range(depth):
        lam_init = 0.8 - 0.6 * math.exp(-0.3 * l)
        wl = w_in[l]
        w_a_in = jnp.concatenate([wl[:, r_u:r_kd], wl[:, r_gd:r_ks], wl[:, r_gs:]], axis=1).astype(BF16)
        w_b_in = jnp.concatenate([wl[:, r_kd:r_gd], wl[:, r_ks:r_gs]], axis=1).astype(BF16)
        wo = w_out[l].astype(BF16)
        w_a, w_d, w_s = wo[:w_ssm], wo[w_ssm:w_ssm + w_diff], wo[w_ssm + w_diff:]
        wg = w_glu[l].astype(BF16)
        q_gain = jnp.tile(diff_qn_g[l], 2).reshape(1, HEAD_DIM)
        k_gain = jnp.tile(diff_kn_g[l], 2).reshape(1, HEAD_DIM)
        ng = norm_g[l].reshape(1, d)
        bbd, cbd, ar_l, ai_l = _s5_layouts(ar_all[l], ai_all[l], bbr_all[l], bbi_all[l],
                                           ssm_c_re[l], ssm_c_im[l])
        d_skip = ssm_d[l].reshape(1, w_ssm)
        lam_vecs = [v[l].reshape(1, DIFF_HALF) for v in (diff_lq1, diff_lk1, diff_lq2, diff_lk2)]
        diff_extra = lam_vecs + [diff_subln_g[l].reshape(1, HEAD_DIM), q_gain, pmat]

        proj = _proj(xp, ng, w_a_in, tm=tm_p, tn=tn)
        stacks = _kvproj_prompt(xp, ng, w_b_in, k_gain, pmat, stacks, layer=l, seq=t, tm=tm_p)
        proj3 = proj.reshape(bp, t, cols_a)
        u = proj3[:, :, :w_ssm].reshape(bp, nseg, t // nseg, w_ssm)
        u_perm = jnp.swapaxes(u, 1, 2).reshape(bp, t, w_ssm)
        y2p, ht = _s5_prompt(u_perm, bbd, cbd, ar_l, ai_l, d_skip, nseg=nseg)
        y2 = jnp.swapaxes(y2p.reshape(bp, t // nseg, nseg, w_ssm), 1, 2).reshape(bp * t, w_ssm)
        o_d = _prompt_attention(
            functools.partial(_diff_prompt_kernel, blk=blk, lam_init=lam_init), proj3,
            stacks[0], stacks[1], diff_extra, [pltpu.VMEM((t // blk, HEAD_DIM, blk), BF16)],
            layer=l, blk=blk,
            col_q=c_qd // HEAD_DIM, col_g=c_gd // HEAD_DIM, name="diff_prompt")
        o_s = _prompt_attention(
            functools.partial(_sb_prompt_kernel, blk=blk), proj3, stacks[2], stacks[3], [], [],
            layer=l, blk=blk, col_q=c_qs // HEAD_DIM, col_g=c_gs // HEAD_DIM, name="sb_prompt")
        xp = _outproj(y2, proj, o_d.reshape(bp * t, w_diff), o_s.reshape(bp * t, w_sb), xp,
                      wg, w_a, w_d, w_s, tm=tmo_p, ga_block=c_ga // w_ssm)
        ht = ht.reshape(bp, nch, 2, SSM_CHUNK, n)
        ssm_p.append((ht[:, :, 0].reshape(bp, g, n), ht[:, :, 1].reshape(bp, g, n)))

        proj_s = _proj(xs, ng, w_a_in, tm=bs, tn=tn)
        kv_s = _kvproj_step(xs, ng, w_b_in, k_gain, pmat)
        proj_s3 = proj_s.reshape(bs, 1, cols_a)
        kv_s3 = kv_s.reshape(bs, 1, 4 * w_diff)
        h0 = jnp.concatenate([state_ssm_re[l].reshape(bs, nch, half),
                              state_ssm_im[l].reshape(bs, nch, half)], axis=-1)
        y2s, hts = _s5_step(proj_s, jnp.swapaxes(h0, 0, 1), bbd, cbd, ar_l, ai_l, d_skip)
        od_s = _decode_attention(
            functools.partial(_diff_decode_kernel, n_pp=n_pp, col_q=c_qd, col_g=c_gd, n_heads=dh,
                              lam_init=lam_init),
            page_table, [proj_s3, kv_s3[:, :, :2 * w_diff]], ck_d, cv_d, diff_extra,
            [pltpu.VMEM((dh, SUBLANES, HEAD_DIM), BF16), pltpu.VMEM((dh * SUBLANES, 1), F32),
             pltpu.VMEM((dh * SUBLANES, 1), F32), pltpu.VMEM((dh * SUBLANES, HEAD_DIM), F32)],
            layer=l, n_pp=n_pp, reverse=False, name="diff_decode")
        os_s = _sb_decode(page_table, proj_s3, ck_s, cv_s, layer=l, n_pp=n_pp_sb, col_q=c_qs, col_g=c_gs)
        xs = _outproj(y2s, proj_s, od_s.reshape(bs, w_diff), os_s.reshape(bs, w_sb), xs,
                      wg, w_a, w_d, w_s, tm=bs, ga_block=c_ga // w_ssm)
        hts = jnp.swapaxes(hts, 0, 1).reshape(bs, nch, 2, SSM_CHUNK, n)
        kv4 = kv_s.reshape(bs, 1, 4, dh, HEAD_DIM)
        rows_s.append((kv4[:, :, 0], kv4[:, :, 1], kv4[:, :, 2], kv4[:, :, 3],
                       hts[:, :, 0].reshape(bs, g, n), hts[:, :, 1].reshape(bs, g, n)))

    kv_p = [jnp.transpose(s, (0, 1, 3, 2, 4)) for s in stacks]
    ssm_p = [jnp.stack(r) for r in zip(*ssm_p)]
    outs_s = [jnp.stack(r) for r in zip(*rows_s)]
    return (xp.reshape(bp, t, d), xs.reshape(bs, 1, d), *kv_p, *ssm_p, *outs_s)
```

```python
import functools
import math

import jax
import jax.numpy as jnp
from jax import lax
from jax.experimental import pallas as pl
from jax.experimental.pallas import tpu as pltpu

F32 = jnp.float32
BF16 = jnp.bfloat16

EPS = 1e-6
LANES = 128
SUBLANES = 8
HEAD_DIM = 128
DIFF_HALF = HEAD_DIM // 2
SSM_P = 16
SSM_N = 64
SSM_CHUNK = LANES // SSM_P
SB_EXIT_LOG = -110.0
FLASH_UNROLL = 8
LOG2_E = 1.4426950408889634
VMEM_LIMIT = 56 * 1024 * 1024


def _params(sem):
    return pltpu.CompilerParams(dimension_semantics=sem, vmem_limit_bytes=VMEM_LIMIT)


def _split_hi_lo(t):
    hi = t.astype(BF16)
    lo = (t - hi.astype(F32)).astype(BF16)
    return hi, lo


def _dot(a, b):
    return jnp.dot(a, b, preferred_element_type=F32)


def _dot_nt(a, b):
    return lax.dot_general(a, b, (((1,), (1,)), ((), ())), preferred_element_type=F32)


def _silu(x):
    return x * (1.0 / (1.0 + jnp.exp(-x)))


def _half_rms(y, pm, gain):
    hi, lo = _split_hi_lo(y * y)
    ms = _dot(hi, pm) + _dot(lo, pm)
    return y * lax.rsqrt(ms + EPS) * gain


def _normed_rows(x_ref, g_ref, xn_ref):
    @pl.when(pl.program_id(1) == 0)
    def _():
        x = x_ref[...]
        ms = jnp.mean(x * x, axis=-1, keepdims=True)
        xn_ref[...] = (x * lax.rsqrt(ms + EPS) * g_ref[...]).astype(BF16)


def _proj_kernel(x_ref, g_ref, w_ref, qg_ref, pm_ref, o_ref, xn_ref, *, qk_tiles):
    _normed_rows(x_ref, g_ref, xn_ref)
    y = _dot(xn_ref[...], w_ref[...])
    j = pl.program_id(1)
    for tile, slabs in qk_tiles.items():
        @pl.when(j == tile)
        def _(slabs=slabs):
            for c in range(y.shape[1] // LANES):
                yc = y[:, c * LANES:(c + 1) * LANES]
                if c in slabs:
                    yc = _half_rms(yc, pm_ref[...], qg_ref[...])
                o_ref[:, c * LANES:(c + 1) * LANES] = yc

    plain = functools.reduce(jnp.logical_and, [j != tile for tile in qk_tiles])

    @pl.when(plain)
    def _():
        o_ref[...] = y


def _proj(x2, norm_g, w_bf, q_gain, pmat, *, q_cols, tm, tn):
    rows, d = x2.shape
    ncol = w_bf.shape[1]
    qk_tiles = {}
    for col in range(q_cols[0], q_cols[1], LANES):
        qk_tiles.setdefault(col // tn, []).append((col % tn) // LANES)
    return pl.pallas_call(
        functools.partial(_proj_kernel, qk_tiles=qk_tiles),
        grid=(rows // tm, ncol // tn),
        in_specs=[
            pl.BlockSpec((tm, d), lambda i, j: (i, 0)),
            pl.BlockSpec((1, d), lambda i, j: (0, 0)),
            pl.BlockSpec((d, tn), lambda i, j: (0, j)),
            pl.BlockSpec((1, HEAD_DIM), lambda i, j: (0, 0)),
            pl.BlockSpec((LANES, LANES), lambda i, j: (0, 0)),
        ],
        out_specs=pl.BlockSpec((tm, tn), lambda i, j: (i, j)),
        out_shape=jax.ShapeDtypeStruct((rows, ncol), F32),
        scratch_shapes=[pltpu.VMEM((tm, d), BF16)],
        compiler_params=_params(("parallel", "arbitrary")),
        name="proj",
    )(x2, norm_g, w_bf, q_gain, pmat)


def _kv_heads(y, kg_ref, pm_ref, normed):
    heads = [y[:, h * HEAD_DIM:(h + 1) * HEAD_DIM] for h in range(y.shape[1] // HEAD_DIM)]
    if normed:
        heads = [_half_rms(yh, pm_ref[...], kg_ref[...]) for yh in heads]
    return heads


def _kvproj_prompt_kernel(x_ref, g_ref, w_ref, kg_ref, pm_ref, a0, a1, a2, a3,
                          kd_ref, vd_ref, ks_ref, vs_ref, xn_ref):
    del a0, a1, a2, a3
    _normed_rows(x_ref, g_ref, xn_ref)
    j = pl.program_id(1)
    y = _dot(xn_ref[...], w_ref[...])
    for idx, ref in enumerate((kd_ref, vd_ref, ks_ref, vs_ref)):
        @pl.when(j == idx)
        def _(ref=ref, idx=idx):
            for h, yh in enumerate(_kv_heads(y, kg_ref, pm_ref, idx == 0)):
                ref[h] = yh


def _kvproj_prompt(x2, norm_g, w_bf, k_gain, pmat, stacks, *, layer, seq, tm):
    rows, d = x2.shape
    width = w_bf.shape[1] // 4
    n_heads = width // HEAD_DIM
    tpb = seq // tm
    any_spec = pl.BlockSpec(memory_space=pl.ANY)
    out_spec = pl.BlockSpec((None, None, n_heads, tm, HEAD_DIM),
                            lambda i, j: (layer, i // tpb, 0, i % tpb, 0))
    return pl.pallas_call(
        _kvproj_prompt_kernel,
        grid=(rows // tm, 4),
        in_specs=[
            pl.BlockSpec((tm, d), lambda i, j: (i, 0)),
            pl.BlockSpec((1, d), lambda i, j: (0, 0)),
            pl.BlockSpec((d, width), lambda i, j: (0, j)),
            pl.BlockSpec((1, HEAD_DIM), lambda i, j: (0, 0)),
            pl.BlockSpec((LANES, LANES), lambda i, j: (0, 0)),
            any_spec, any_spec, any_spec, any_spec,
        ],
        out_specs=[out_spec] * 4,
        out_shape=[jax.ShapeDtypeStruct(s.shape, s.dtype) for s in stacks],
        input_output_aliases={5: 0, 6: 1, 7: 2, 8: 3},
        scratch_shapes=[pltpu.VMEM((tm, d), BF16)],
        compiler_params=_params(("parallel", "arbitrary")),
        name="kvproj_prompt",
    )(x2, norm_g, w_bf, k_gain, pmat, *stacks)


def _kvproj_step_kernel(x_ref, g_ref, w_ref, kg_ref, pm_ref, o_ref, xn_ref):
    _normed_rows(x_ref, g_ref, xn_ref)
    j = pl.program_id(1)
    y = _dot(xn_ref[...], w_ref[...])
    for normed in (True, False):
        @pl.when((j == 0) == normed)
        def _(normed=normed):
            for h, yh in enumerate(_kv_heads(y, kg_ref, pm_ref, normed)):
                o_ref[:, h * HEAD_DIM:(h + 1) * HEAD_DIM] = yh


def _kvproj_step(x2, norm_g, w_bf, k_gain, pmat):
    rows, d = x2.shape
    ncol = w_bf.shape[1]
    width = ncol // 4
    return pl.pallas_call(
        _kvproj_step_kernel,
        grid=(1, 4),
        in_specs=[
            pl.BlockSpec((rows, d), lambda i, j: (i, 0)),
            pl.BlockSpec((1, d), lambda i, j: (0, 0)),
            pl.BlockSpec((d, width), lambda i, j: (0, j)),
            pl.BlockSpec((1, HEAD_DIM), lambda i, j: (0, 0)),
            pl.BlockSpec((LANES, LANES), lambda i, j: (0, 0)),
        ],
        out_specs=pl.BlockSpec((rows, width), lambda i, j: (i, j)),
        out_shape=jax.ShapeDtypeStruct((rows, ncol), F32),
        scratch_shapes=[pltpu.VMEM((rows, d), BF16)],
        compiler_params=_params(("parallel", "arbitrary")),
        name="kvproj_step",
    )(x2, norm_g, w_bf, k_gain, pmat)


def _s5_prep_kernel(lr_ref, li_ref, ldt_ref, br_ref, bi_ref, ar_ref, ai_ref, bbr_ref, bbi_ref):
    lr, li = lr_ref[...], li_ref[...]
    dt = jnp.exp(ldt_ref[...])
    mag = jnp.exp(lr * dt)
    ar, ai = mag * jnp.cos(li * dt), mag * jnp.sin(li * dt)
    den = lr * lr + li * li
    kr = ((ar - 1.0) * lr + ai * li) / den
    ki = (ai * lr - (ar - 1.0) * li) / den
    br, bi = br_ref[...], bi_ref[...]
    ar_ref[...] = ar
    ai_ref[...] = ai
    bbr_ref[...] = kr * br - ki * bi
    bbi_ref[...] = kr * bi + ki * br


def _s5_prep(lam_re, lam_im, log_dt, b_re, b_im):
    dep, g, n = lam_re.shape
    p = b_re.shape[-1]
    rows = dep * g * p

    def rep(a):
        return jnp.broadcast_to(a[:, :, None, :], (dep, g, p, n)).reshape(rows, n)

    ldt = jnp.broadcast_to(log_dt[:, :, None, None], (dep, g, p, n)).reshape(rows, n)
    brt = jnp.swapaxes(b_re, 2, 3).reshape(rows, n)
    bit = jnp.swapaxes(b_im, 2, 3).reshape(rows, n)
    shp = jax.ShapeDtypeStruct((rows, n), F32)
    ar, ai, bbr, bbi = pl.pallas_call(
        _s5_prep_kernel, out_shape=(shp, shp, shp, shp), name="s5_prep",
    )(rep(lam_re), rep(lam_im), ldt, brt, bit)
    ar = ar.reshape(dep, g, p, n)[:, :, 0]
    ai = ai.reshape(dep, g, p, n)[:, :, 0]
    return ar, ai, bbr.reshape(dep, g, p, n), bbi.reshape(dep, g, p, n)


def _s5_layouts(ar, ai, bbr, bbi, c_re, c_im):
    g, n = ar.shape
    nch = g // SSM_CHUNK
    half = SSM_CHUNK * n
    eye = jnp.eye(SSM_CHUNK, dtype=F32)

    def in_bd(bb):
        bb = bb.reshape(nch, SSM_CHUNK, SSM_P, n)
        return (bb[:, :, :, None, :] * eye[None, :, None, :, None]).reshape(nch, LANES, half)

    def out_bd(c):
        c = jnp.swapaxes(c.reshape(nch, SSM_CHUNK, SSM_P, n), 2, 3)
        return (c[:, :, :, None, :] * eye[None, :, None, :, None]).reshape(nch, half, LANES)

    bbd = jnp.concatenate([in_bd(bbr), in_bd(bbi)], axis=-1).astype(BF16)
    cbd = jnp.concatenate([out_bd(c_re), -out_bd(c_im)], axis=1).astype(BF16)
    return bbd, cbd, ar.reshape(nch, 1, half), ai.reshape(nch, 1, half)


def _cmul(ar, ai, br, bi):
    return ar * br - ai * bi, ar * bi + ai * br


def _s5_prompt_kernel(u_ref, bbd_ref, cbd_ref, ar_ref, ai_ref, d_ref, y_ref, ht_ref, x_scr,
                      *, nseg, seglen, unroll):
    half = ar_ref.shape[-1]
    jc = min(seglen, 64)
    rc = jc * nseg

    def fill(c, carry):
        r0 = pl.multiple_of(c * rc, rc)
        x = _dot(u_ref[pl.ds(r0, rc), :].astype(BF16), bbd_ref[...])
        x_scr[pl.ds(c * jc, jc)] = x.reshape(jc, nseg, 2 * half)
        return carry

    lax.fori_loop(0, seglen // jc, fill, 0)
    ar = jnp.broadcast_to(ar_ref[...], (nseg, half))
    ai = jnp.broadcast_to(ai_ref[...], (nseg, half))

    def scan_step(j, carry):
        hr, hi = carry
        pr, pi = _cmul(ar, ai, hr, hi)
        hr = pr + x_scr[j, :, :half]
        hi = pi + x_scr[j, :, half:]
        x_scr[j, :, :half] = hr
        x_scr[j, :, half:] = hi
        return hr, hi

    zero = jnp.zeros((nseg, half), F32)
    er, ei = lax.fori_loop(0, seglen, scan_step, (zero, zero), unroll=unroll)

    sr, si = ar_ref[...], ai_ref[...]
    for _ in range(int(math.log2(seglen))):
        sr, si = _cmul(sr, si, sr, si)

    cin_r, cin_i = [jnp.zeros((1, half), F32)], [jnp.zeros((1, half), F32)]
    for s in range(nseg):
        pr, pi = _cmul(sr, si, cin_r[-1], cin_i[-1])
        cin_r.append(er[s:s + 1] + pr)
        cin_i.append(ei[s:s + 1] + pi)
    ht_ref[:, :half] = cin_r[-1]
    ht_ref[:, half:] = cin_i[-1]
    cr = jnp.concatenate(cin_r[:-1], axis=0)
    ci = jnp.concatenate(cin_i[:-1], axis=0)

    def fix_step(j, carry):
        pr, pi = carry
        dr, di = _cmul(pr, pi, cr, ci)
        x_scr[j, :, :half] = x_scr[j, :, :half] + dr
        x_scr[j, :, half:] = x_scr[j, :, half:] + di
        return _cmul(pr, pi, ar, ai)

    lax.fori_loop(0, seglen, fix_step, (ar, ai), unroll=unroll)

    def drain(c, carry):
        r0 = pl.multiple_of(c * rc, rc)
        h = x_scr[pl.ds(c * jc, jc)].reshape(rc, 2 * half)
        y = _dot(h.astype(BF16), cbd_ref[...]) + d_ref[...] * u_ref[pl.ds(r0, rc), :]
        y_ref[pl.ds(r0, rc), :] = jax.nn.gelu(y)
        return carry

    lax.fori_loop(0, seglen // jc, drain, 0)


def _s5_prompt(u_perm, bbd, cbd, ar_l, ai_l, d_skip, *, nseg):
    b, t, w = u_perm.shape
    nch, _, cols = bbd.shape
    half = cols // 2
    seglen = t // nseg
    kern = functools.partial(_s5_prompt_kernel, nseg=nseg, seglen=seglen, unroll=min(8, seglen))
    return pl.pallas_call(
        kern,
        grid=(b, nch),
        in_specs=[
            pl.BlockSpec((None, t, LANES), lambda i, c: (i, 0, c)),
            pl.BlockSpec((None, LANES, cols), lambda i, c: (c, 0, 0)),
            pl.BlockSpec((None, cols, LANES), lambda i, c: (c, 0, 0)),
            pl.BlockSpec((None, 1, half), lambda i, c: (c, 0, 0)),
            pl.BlockSpec((None, 1, half), lambda i, c: (c, 0, 0)),
            pl.BlockSpec((1, LANES), lambda i, c: (0, c)),
        ],
        out_specs=[
            pl.BlockSpec((None, t, LANES), lambda i, c: (i, 0, c)),
            pl.BlockSpec((None, None, 1, cols), lambda i, c: (i, c, 0, 0)),
        ],
        out_shape=[jax.ShapeDtypeStruct((b, t, w), F32),
                   jax.ShapeDtypeStruct((b, nch, 1, cols), F32)],
        scratch_shapes=[pltpu.VMEM((seglen, nseg, cols), F32)],
        compiler_params=_params(("parallel", "arbitrary")),
        name="s5_prompt",
    )(u_perm, bbd, cbd, ar_l, ai_l, d_skip)


def _s5_step_kernel(p_ref, h0_ref, bbd_ref, cbd_ref, ar_ref, ai_ref, d_ref, y_ref, ht_ref, *, nch):
    half = ar_ref.shape[-1]
    for c in range(nch):
        u = p_ref[:, c * LANES:(c + 1) * LANES]
        x = _dot(u.astype(BF16), bbd_ref[c])
        h0r, h0i = h0_ref[c, :, :half], h0_ref[c, :, half:]
        pr, pi = _cmul(ar_ref[c], ai_ref[c], h0r, h0i)
        hr = pr + x[:, :half]
        hi = pi + x[:, half:]
        ht_ref[c, :, :half] = hr
        ht_ref[c, :, half:] = hi
        h = jnp.concatenate([hr, hi], axis=-1)
        y = _dot(h.astype(BF16), cbd_ref[c]) + d_ref[:, c * LANES:(c + 1) * LANES] * u
        y_ref[:, c * LANES:(c + 1) * LANES] = jax.nn.gelu(y)


def _s5_step(proj_s, h0_l, bbd, cbd, ar_l, ai_l, d_skip):
    b = proj_s.shape[0]
    nch, _, cols = bbd.shape
    w = nch * LANES
    kern = functools.partial(_s5_step_kernel, nch=nch)
    return pl.pallas_call(
        kern,
        out_shape=[jax.ShapeDtypeStruct((b, w), F32), jax.ShapeDtypeStruct((nch, b, cols), F32)],
        compiler_params=pltpu.CompilerParams(vmem_limit_bytes=VMEM_LIMIT),
        name="s5_step",
    )(proj_s, h0_l, bbd, cbd, ar_l, ai_l, d_skip)


def _run_blocks(ks, scores, update, carry):
    s_next = scores(ks[0])
    for idx, k in enumerate(ks):
        s_cur = s_next
        if idx + 1 < len(ks):
            s_next = scores(ks[idx + 1])
        carry = update(k, s_cur, carry)
    return carry


def _loop_blocks(n, scores, update, carry, unroll):
    carry = lax.fori_loop(
        0, n // unroll,
        lambda i, c: _run_blocks([unroll * i + u for u in range(unroll)], scores, update, c), carry)
    base = (n // unroll) * unroll
    rem = n - base
    width = unroll // 2
    while width >= 1:
        take = (rem & width) != 0
        carry = lax.cond(
            take,
            lambda c, base=base, width=width: _run_blocks([base + u for u in range(width)],
                                                          scores, update, c),
            lambda c: c, carry)
        base = base + jnp.where(take, width, 0)
        width //= 2
    return carry


def _diff_lambda(lq1_ref, lk1_ref, lq2_ref, lk2_ref, lam_init):
    s1 = jnp.sum(lq1_ref[...] * lk1_ref[...], axis=-1, keepdims=True)
    s2 = jnp.sum(lq2_ref[...] * lk2_ref[...], axis=-1, keepdims=True)
    return jnp.exp(s1) - jnp.exp(s2) + lam_init


def _diff_prompt_kernel(q_ref, k_ref, v_ref, g_ref, lq1_ref, lk1_ref, lq2_ref, lk2_ref, sg_ref,
                        o_ref, vt_scr, *, blk, lam_init):
    qi = pl.program_id(2)

    @pl.when(qi == 0)
    def _():
        for c in range(vt_scr.shape[0]):
            vt_scr[c] = v_ref[c * blk:(c + 1) * blk, :].T.astype(BF16)

    q = q_ref[...] * (DIFF_HALF ** -0.5 * LOG2_E)
    lane = lax.broadcasted_iota(jnp.int32, q.shape, 1)
    q2 = jnp.concatenate([jnp.where(lane < DIFF_HALF, q, 0.0),
                          jnp.where(lane >= DIFF_HALF, q, 0.0)], axis=0).astype(BF16)

    def scores(ki):
        start = pl.multiple_of(ki * blk, blk)
        return _dot_nt(k_ref[pl.ds(start, blk), :].astype(BF16), q2)

    def update(ki, s, carry):
        m, l, acc = carry
        m_new = jnp.maximum(m, jnp.max(s, axis=0, keepdims=True))
        alpha = jnp.exp2(m - m_new)
        p = jnp.exp2(s - m_new)
        l = alpha * l + jnp.sum(p, axis=0, keepdims=True)
        acc = alpha * acc + _dot(vt_scr[ki], p.astype(BF16))
        return m_new, l, acc

    init = (jnp.full((1, 2 * blk), -jnp.inf, F32), jnp.zeros((1, 2 * blk), F32),
            jnp.zeros((HEAD_DIM, 2 * blk), F32))
    s = scores(qi)
    key = lax.broadcasted_iota(jnp.int32, s.shape, 0)
    qry = lax.broadcasted_iota(jnp.int32, s.shape, 1)
    qry = jnp.where(qry >= blk, qry - blk, qry)
    carry = update(qi, jnp.where(key <= qry, s, -jnp.inf), init)
    m, l, acc = _loop_blocks(qi, scores, update, carry, FLASH_UNROLL)

    lam = _diff_lambda(lq1_ref, lk1_ref, lq2_ref, lk2_ref, lam_init)
    on = acc * (1.0 / l)
    o = (on[:, :blk] - lam * on[:, blk:]).T
    ms = jnp.mean(o * o, axis=-1, keepdims=True)
    o = o * lax.rsqrt(ms + EPS) * sg_ref[...] * (1.0 - lam_init)
    o_ref[...] = (o * _silu(g_ref[...])).astype(BF16)


def _sb_scores(z):
    sp = jnp.log1p(jnp.exp(-jnp.abs(z)))
    return jnp.minimum(z, 0.0) - sp, -jnp.maximum(z, 0.0) - sp


def _strict_lower(n):
    r = lax.broadcasted_iota(jnp.int32, (n, n), 0)
    c = lax.broadcasted_iota(jnp.int32, (n, n), 1)
    return jnp.where(r > c, 1.0, 0.0).astype(BF16)


def _sum_later(lk, tri):
    hi, lo = _split_hi_lo(lk)
    return _dot(hi, tri) + _dot(lo, tri)


def _sb_prompt_kernel(q_ref, k_ref, v_ref, g_ref, o_ref, *, blk):
    qa = 2 * pl.program_id(2)
    q_all = (q_ref[...] * (HEAD_DIM ** -0.5)).astype(BF16)
    q_half = (q_all[:blk], q_all[blk:])
    tri = _strict_lower(blk)

    def scores(w, ki):
        start = pl.multiple_of(ki * blk, blk)
        return _dot_nt(q_half[w], k_ref[pl.ds(start, blk), :].astype(BF16))

    def sums(z, masked):
        lb, lk = _sb_scores(z)
        if masked:
            row = lax.broadcasted_iota(jnp.int32, lb.shape, 0)
            col = lax.broadcasted_iota(jnp.int32, lb.shape, 1)
            lk = jnp.where(col < row, lk, 0.0)
        return lb, _sum_later(lk, tri), jnp.sum(lk, axis=-1, keepdims=True)

    def accumulate(ki, parts, carry, masked):
        lb, later, total = parts
        r, acc = carry
        a = jnp.exp(lb + (r + later))
        if masked:
            row = lax.broadcasted_iota(jnp.int32, lb.shape, 0)
            col = lax.broadcasted_iota(jnp.int32, lb.shape, 1)
            a = jnp.where(col < row, a, 0.0)
        start = pl.multiple_of(ki * blk, blk)
        acc = acc + _dot(a.astype(BF16), v_ref[pl.ds(start, blk), :].astype(BF16))
        return r + total, acc

    def run(specs, carries):
        carries = list(carries)
        zs = [scores(w, ki) for w, ki, _ in specs]
        parts = [sums(z, masked) for z, (_, _, masked) in zip(zs, specs)]
        for (w, ki, masked), part in zip(specs, parts):
            carries[w] = accumulate(ki, part, carries[w], masked)
        return tuple(carries)

    zero = (jnp.zeros((blk, 1), F32), jnp.zeros((blk, HEAD_DIM), F32))
    head = [(0, qa, True), (1, qa + 1, True)]
    carries = lax.cond(qa >= 1,
                       lambda c: run(head + [(0, qa - 1, False), (1, qa, False)], c),
                       lambda c: run(head + [(1, qa, False)], c), (zero, zero))

    def walk(w, first, carry):
        def body(c):
            ki, _, r, acc = c
            r, acc = accumulate(ki, sums(scores(w, ki), False), (r, acc), False)
            return ki - 1, jnp.max(r), r, acc

        def live(c):
            return jnp.logical_and(c[0] >= 0, c[1] > SB_EXIT_LOG)

        r, acc = carry
        return lax.while_loop(live, body, (first, jnp.max(r), r, acc))[3]

    g = g_ref[...]
    o_ref[:blk] = (walk(0, qa - 2, carries[0]) * _silu(g[:blk])).astype(BF16)
    o_ref[blk:] = (walk(1, qa - 1, carries[1]) * _silu(g[blk:])).astype(BF16)


def _prompt_attention(kern, proj3, k_stack, v_stack, extra, scratch, *, layer, blk, col_q, col_g,
                      name):
    b, t, _ = proj3.shape
    n_heads = k_stack.shape[2]
    qspec = lambda c0: pl.BlockSpec((None, blk, HEAD_DIM), lambda i, h, q: (i, q, c0 + h))
    kvspec = pl.BlockSpec((None, None, None, t, HEAD_DIM), lambda i, h, q: (layer, i, h, 0, 0))
    small = [pl.BlockSpec(e.shape, lambda i, h, q: (0, 0)) for e in extra]
    return pl.pallas_call(
        kern,
        grid=(b, n_heads, t // blk),
        in_specs=[qspec(col_q), kvspec, kvspec, qspec(col_g)] + small,
        out_specs=pl.BlockSpec((None, blk, HEAD_DIM), lambda i, h, q: (i, q, h)),
        out_shape=jax.ShapeDtypeStruct((b, t, n_heads * HEAD_DIM), BF16),
        scratch_shapes=scratch,
        compiler_params=_params(("parallel", "parallel", "arbitrary")),
        name=name,
    )(proj3, k_stack, v_stack, proj3, *extra)


def _page_scores(wq_scr, k_refs, n_heads):
    return jnp.concatenate([
        jnp.concatenate([_dot_nt(wq_scr[h], k[h].astype(BF16)) for k in k_refs], axis=-1)
        for h in range(n_heads)], axis=0)


def _page_values(p, v_refs, n_heads):
    out = []
    for h in range(n_heads):
        ph = p[h * SUBLANES:(h + 1) * SUBLANES].astype(BF16)
        pv = _dot(ph[:, :LANES], v_refs[0][h].astype(BF16))
        for j in range(1, len(v_refs)):
            pv = pv + _dot(ph[:, j * LANES:(j + 1) * LANES], v_refs[j][h].astype(BF16))
        out.append(pv)
    return jnp.concatenate(out, axis=0)


def _diff_decode_kernel(pt_ref, p_ref, kv_ref, lq1_ref, lk1_ref, lq2_ref, lk2_ref, sg_ref,
                        *rest, n_pp, col_q, col_g, n_heads, lam_init):
    del pt_ref
    k_refs, v_refs = rest[:n_pp], rest[n_pp:2 * n_pp]
    o_ref, wq_scr, m_scr, l_scr, acc_scr = rest[2 * n_pp:]
    width = n_heads * HEAD_DIM
    s_idx = pl.program_id(1)

    @pl.when(s_idx == 0)
    def _():
        row = lax.broadcasted_iota(jnp.int32, (SUBLANES, HEAD_DIM), 0)
        lane = lax.broadcasted_iota(jnp.int32, (SUBLANES, HEAD_DIM), 1)
        keep = jnp.logical_or(jnp.logical_and(row == 0, lane < DIFF_HALF),
                              jnp.logical_and(row == 1, lane >= DIFF_HALF))
        for h in range(n_heads):
            qh = p_ref[:, col_q + h * HEAD_DIM:col_q + (h + 1) * HEAD_DIM] * (DIFF_HALF ** -0.5)
            wq_scr[h] = jnp.where(keep, jnp.broadcast_to(qh, (SUBLANES, HEAD_DIM)), 0.0).astype(BF16)
        m_scr[...] = jnp.full(m_scr.shape, -jnp.inf, F32)
        l_scr[...] = jnp.zeros(l_scr.shape, F32)
        acc_scr[...] = jnp.zeros(acc_scr.shape, F32)

    s = _page_scores(wq_scr, k_refs, n_heads)
    m_old = m_scr[...]
    m_new = jnp.maximum(m_old, jnp.max(s, axis=-1, keepdims=True))
    alpha = jnp.exp(m_old - m_new)
    p = jnp.exp(s - m_new)
    l_scr[...] = alpha * l_scr[...] + jnp.sum(p, axis=-1, keepdims=True)
    acc_scr[...] = alpha * acc_scr[...] + _page_values(p, v_refs, n_heads)
    m_scr[...] = m_new

    @pl.when(s_idx == pl.num_programs(1) - 1)
    def _():
        lam = _diff_lambda(lq1_ref, lk1_ref, lq2_ref, lk2_ref, lam_init)
        outs = []
        for h in range(n_heads):
            rows = slice(h * SUBLANES, (h + 1) * SUBLANES)
            k_new = kv_ref[:, h * HEAD_DIM:(h + 1) * HEAD_DIM].astype(BF16).astype(F32)
            v_new = kv_ref[:, width + h * HEAD_DIM:width + (h + 1) * HEAD_DIM].astype(BF16).astype(F32)
            s_new = jnp.sum(wq_scr[h].astype(F32) * k_new, axis=-1, keepdims=True)
            m_old = m_scr[rows]
            m_fin = jnp.maximum(m_old, s_new)
            a_fin = jnp.exp(m_old - m_fin)
            p_new = jnp.exp(s_new - m_fin)
            l_fin = a_fin * l_scr[rows] + p_new
            on = (a_fin * acc_scr[rows] + p_new.astype(BF16).astype(F32) * v_new) / l_fin
            o = on[0:1] - lam * on[1:2]
            ms = jnp.mean(o * o, axis=-1, keepdims=True)
            o = o * lax.rsqrt(ms + EPS) * sg_ref[...] * (1.0 - lam_init)
            g = p_ref[:, col_g + h * HEAD_DIM:col_g + (h + 1) * HEAD_DIM]
            outs.append(o * _silu(g))
        o_ref[...] = jnp.concatenate(outs, axis=-1).astype(BF16)


def _sb_decode_kernel(pt_ref, p_ref, ck_ref, cv_ref, o_ref, wq_scr, r_scr, acc_scr, kbuf, vbuf, sem,
                      *, layer, n_pp, col_q, col_g, n_heads):
    seq = pl.program_id(0)
    n_pages = pt_ref.shape[1]
    n_groups = n_pages // n_pp

    def copies(g, slot):
        out = []
        for j in range(n_pp):
            page = pt_ref[seq, n_pages - 1 - (g * n_pp + j)]
            out.append(pltpu.make_async_copy(ck_ref.at[layer, page], kbuf.at[slot, j], sem.at[slot, 0, j]))
            out.append(pltpu.make_async_copy(cv_ref.at[layer, page], vbuf.at[slot, j], sem.at[slot, 1, j]))
        return out

    for c in copies(0, 0):
        c.start()
    for h in range(n_heads):
        qh = p_ref[:, col_q + h * HEAD_DIM:col_q + (h + 1) * HEAD_DIM] * (HEAD_DIM ** -0.5)
        wq_scr[h] = jnp.broadcast_to(qh, (SUBLANES, HEAD_DIM)).astype(BF16)
    r_scr[...] = jnp.zeros(r_scr.shape, F32)
    acc_scr[...] = jnp.zeros(acc_scr.shape, F32)

    def group(carry):
        g, _ = carry
        slot = g & 1

        @pl.when(g + 1 < n_groups)
        def _():
            for c in copies(g + 1, 1 - slot):
                c.start()

        for c in copies(g, slot):
            c.wait()
        k_refs = [kbuf.at[slot, j] for j in range(n_pp)]
        v_refs = [vbuf.at[slot, j] for j in range(n_pp)]
        lb, lk = _sb_scores(_page_scores(wq_scr, k_refs, n_heads))
        hi, lo = _split_hi_lo(lk)
        r_i = lax.broadcasted_iota(jnp.int32, (LANES, 2 * LANES), 0)
        c_i = lax.broadcasted_iota(jnp.int32, (LANES, 2 * LANES), 1)
        tri_ones = jnp.where(jnp.logical_or(c_i >= LANES, r_i > c_i), 1.0, 0.0).astype(BF16)
        r = r_scr[...]
        weights = []
        for j in range(n_pp):
            sl = slice(j * LANES, (j + 1) * LANES)
            sums = _dot(hi[:, sl], tri_ones) + _dot(lo[:, sl], tri_ones)
            weights.append(jnp.exp(lb[:, sl] + (r + sums[:, :LANES])))
            r = r + sums[:, LANES:]
        r_scr[...] = r
        acc_scr[...] = acc_scr[...] + _page_values(jnp.concatenate(weights, axis=-1), v_refs, n_heads)
        return g + 1, jnp.max(r)

    def live(carry):
        return jnp.logical_and(carry[0] < n_groups, carry[1] > SB_EXIT_LOG)

    g_end, _ = lax.while_loop(live, group, (jnp.int32(0), jnp.float32(0.0)))

    @pl.when(g_end < n_groups)
    def _():
        for c in copies(g_end, g_end & 1):
            c.wait()

    outs = []
    for h in range(n_heads):
        gate = p_ref[:, col_g + h * HEAD_DIM:col_g + (h + 1) * HEAD_DIM]
        outs.append(acc_scr[h * SUBLANES:h * SUBLANES + 1] * _silu(gate))
    o_ref[...] = jnp.concatenate(outs, axis=-1).astype(BF16)


def _sb_decode(page_table, proj_s3, cache_k, cache_v, *, layer, n_pp, col_q, col_g):
    b, _, cols = proj_s3.shape
    _, _, n_heads, page, _ = cache_k.shape
    rows = n_heads * SUBLANES
    kern = functools.partial(_sb_decode_kernel, layer=layer, n_pp=n_pp, col_q=col_q, col_g=col_g,
                             n_heads=n_heads)
    buf = pltpu.VMEM((2, n_pp, n_heads, page, HEAD_DIM), F32)
    grid_spec = pltpu.PrefetchScalarGridSpec(
        num_scalar_prefetch=1,
        grid=(b,),
        in_specs=[pl.BlockSpec((None, 1, cols), lambda i, pt: (i, 0, 0)),
                  pl.BlockSpec(memory_space=pl.ANY), pl.BlockSpec(memory_space=pl.ANY)],
        out_specs=pl.BlockSpec((None, 1, n_heads * HEAD_DIM), lambda i, pt: (i, 0, 0)),
        scratch_shapes=[pltpu.VMEM((n_heads, SUBLANES, HEAD_DIM), BF16), pltpu.VMEM((rows, LANES), F32),
                        pltpu.VMEM((rows, HEAD_DIM), F32), buf, buf,
                        pltpu.SemaphoreType.DMA((2, 2, n_pp))],
    )
    return pl.pallas_call(
        kern,
        grid_spec=grid_spec,
        out_shape=jax.ShapeDtypeStruct((b, 1, n_heads * HEAD_DIM), BF16),
        compiler_params=_params(("arbitrary",)),
        name="sb_decode",
    )(page_table, proj_s3, cache_k, cache_v)


def _decode_attention(kern, page_table, rows3, cache_k, cache_v, extra, scratch, *, layer,
                      n_pp, reverse, name):
    b = rows3[0].shape[0]
    n_pages = page_table.shape[1]
    _, _, n_heads, page, _ = cache_k.shape
    n_steps = n_pages // n_pp

    def page_spec(j):
        def imap(i, s, pt):
            slot = s * n_pp + j
            if reverse:
                slot = n_pages - 1 - slot
            return (layer, pt[i, slot], 0, 0, 0)
        return pl.BlockSpec((None, None, n_heads, page, HEAD_DIM), imap)

    per_seq = [pl.BlockSpec((None, 1, r.shape[2]), lambda i, s, pt: (i, 0, 0)) for r in rows3]
    small = [pl.BlockSpec(e.shape, lambda i, s, pt: (0, 0)) for e in extra]
    pages = [page_spec(j) for j in range(n_pp)]
    grid_spec = pltpu.PrefetchScalarGridSpec(
        num_scalar_prefetch=1,
        grid=(b, n_steps),
        in_specs=per_seq + small + pages + pages,
        out_specs=pl.BlockSpec((None, 1, n_heads * HEAD_DIM), lambda i, s, pt: (i, 0, 0)),
        scratch_shapes=scratch,
    )
    return pl.pallas_call(
        kern,
        grid_spec=grid_spec,
        out_shape=jax.ShapeDtypeStruct((b, 1, n_heads * HEAD_DIM), BF16),
        compiler_params=_params(("parallel", "arbitrary")),
        name=name,
    )(page_table, *rows3, *extra, *([cache_k] * n_pp), *([cache_v] * n_pp))


def _outproj_kernel(y_ref, ga_ref, od_ref, os_ref, x_ref, wg_ref, wa_ref, wd_ref, ws_ref, o_ref):
    y = y_ref[...]
    gate = 1.0 / (1.0 + jnp.exp(-_dot(y.astype(BF16), wg_ref[...])))
    oa = (y * gate * _silu(ga_ref[...])).astype(BF16)
    acc = _dot(oa, wa_ref[...]) + _dot(od_ref[...], wd_ref[...]) + _dot(os_ref[...], ws_ref[...])
    o_ref[...] = x_ref[...] + acc


def _outproj(y2, proj, o_d, o_s, x2, w_glu, w_a, w_d, w_s, *, tm, ga_block):
    rows, d = x2.shape
    wa = y2.shape[1]
    wd, ws = o_d.shape[1], o_s.shape[1]
    row = lambda w: pl.BlockSpec((tm, w), lambda i: (i, 0))
    full = lambda a: pl.BlockSpec(a.shape, lambda i: (0, 0))
    return pl.pallas_call(
        _outproj_kernel,
        grid=(rows // tm,),
        in_specs=[row(wa), pl.BlockSpec((tm, wa), lambda i: (i, ga_block)), row(wd), row(ws), row(d),
                  full(w_glu), full(w_a), full(w_d), full(w_s)],
        out_specs=row(d),
        out_shape=jax.ShapeDtypeStruct((rows, d), F32),
        compiler_params=_params(("parallel",)),
        name="outproj",
    )(y2, proj, o_d, o_s, x2, w_glu, w_a, w_d, w_s)


def _largest_tile(n, cap):
    t = min(n, cap)
    while n % t:
        t //= 2
    return t


def kernel(x_prompt, x_sample, cache_k_diff, cache_v_diff, cache_k_sb, cache_v_sb, state_ssm_re, state_ssm_im, page_table, norm_g, w_in, ssm_lam_re, ssm_lam_im, ssm_log_dt, ssm_b_re, ssm_b_im, ssm_c_re, ssm_c_im, ssm_d, w_glu, diff_qn_g, diff_kn_g, diff_lq1, diff_lk1, diff_lq2, diff_lk2, diff_subln_g, w_out):
    bp, t, d = x_prompt.shape
    bs = x_sample.shape[0]
    depth = w_in.shape[0]
    dh, sh = cache_k_diff.shape[3], cache_k_sb.shape[3]
    g, n = ssm_lam_re.shape[1], ssm_lam_re.shape[2]
    w_ssm = g * SSM_P
    w_diff, w_sb = dh * HEAD_DIM, sh * HEAD_DIM
    assert w_diff == w_sb
    nch = w_ssm // LANES
    half = SSM_CHUNK * n
    widths = [w_ssm, w_ssm] + [w_diff] * 4 + [w_sb] * 4
    offs = [0]
    for wd_ in widths:
        offs.append(offs[-1] + wd_)
    (r_u, r_ga, r_qd, r_kd, r_vd, r_gd, r_qs, r_ks, r_vs, r_gs, ncol) = offs
    assert ncol == w_in.shape[2]
    c_ga, c_qd, c_gd, c_qs, c_gs = w_ssm, 2 * w_ssm, 2 * w_ssm + w_diff, 2 * w_ssm + 2 * w_diff, \
        2 * w_ssm + 2 * w_diff + w_sb
    cols_a = c_gs + w_sb

    tn = 512
    assert cols_a % tn == 0
    tm_p = _largest_tile(t, 1024)
    tmo_p = _largest_tile(bp * t, 512)
    blk = _largest_tile(t, 256)
    nseg = SUBLANES
    n_pp = _largest_tile(page_table.shape[1], 16)
    n_pp_sb = _largest_tile(page_table.shape[1], 2)

    lane_half = jnp.arange(LANES) // DIFF_HALF
    pmat = (lane_half[:, None] == lane_half[None, :]).astype(BF16) * (1.0 / DIFF_HALF)

    ar_all, ai_all, bbr_all, bbi_all = _s5_prep(ssm_lam_re, ssm_lam_im, ssm_log_dt, ssm_b_re, ssm_b_im)

    ck_d, cv_d, ck_s, cv_s = [jnp.transpose(c, (0, 1, 3, 2, 4))
                              for c in (cache_k_diff, cache_v_diff, cache_k_sb, cache_v_sb)]
    stacks = [jnp.zeros((depth, bp, dh, t, HEAD_DIM), F32), jnp.zeros((depth, bp, dh, t, HEAD_DIM), F32),
              jnp.zeros((depth, bp, sh, t, HEAD_DIM), F32), jnp.zeros((depth, bp, sh, t, HEAD_DIM), F32)]

    xp = x_prompt.reshape(bp * t, d)
    xs = x_sample.reshape(bs, d)
    ssm_p, rows_s = [], []
    for l in range(depth):
        lam_init = 0.8 - 0.6 * math.exp(-0.3 * l)
        wl = w_in[l]
        w_a_in = jnp.concatenate([wl[:, r_u:r_kd], wl[:, r_gd:r_ks], wl[:, r_gs:]], axis=1).astype(BF16)
        w_b_in = jnp.concatenate([wl[:, r_kd:r_gd], wl[:, r_ks:r_gs]], axis=1).astype(BF16)
        wo = w_out[l].astype(BF16)
        w_a, w_d, w_s = wo[:w_ssm], wo[w_ssm:w_ssm + w_diff], wo[w_ssm + w_diff:]
        wg = w_glu[l].astype(BF16)
        q_gain = jnp.tile(diff_qn_g[l], 2).reshape(1, HEAD_DIM)
        k_gain = jnp.tile(diff_kn_g[l], 2).reshape(1, HEAD_DIM)
        ng = norm_g[l].reshape(1, d)
        bbd, cbd, ar_l, ai_l = _s5_layouts(ar_all[l], ai_all[l], bbr_all[l], bbi_all[l],
                                           ssm_c_re[l], ssm_c_im[l])
        d_skip = ssm_d[l].reshape(1, w_ssm)
        lam_vecs = [v[l].reshape(1, DIFF_HALF) for v in (diff_lq1, diff_lk1, diff_lq2, diff_lk2)]
        diff_extra = lam_vecs + [diff_subln_g[l].reshape(1, HEAD_DIM)]

        proj = _proj(xp, ng, w_a_in, q_gain, pmat, q_cols=(c_qd, c_gd), tm=tm_p, tn=tn)
        stacks = _kvproj_prompt(xp, ng, w_b_in, k_gain, pmat, stacks, layer=l, seq=t, tm=tm_p)
        proj3 = proj.reshape(bp, t, cols_a)
        u = proj3[:, :, :w_ssm].reshape(bp, nseg, t // nseg, w_ssm)
        u_perm = jnp.swapaxes(u, 1, 2).reshape(bp, t, w_ssm)
        y2p, ht = _s5_prompt(u_perm, bbd, cbd, ar_l, ai_l, d_skip, nseg=nseg)
        y2 = jnp.swapaxes(y2p.reshape(bp, t // nseg, nseg, w_ssm), 1, 2).reshape(bp * t, w_ssm)
        o_d = _prompt_attention(
            functools.partial(_diff_prompt_kernel, blk=blk, lam_init=lam_init), proj3,
            stacks[0], stacks[1], diff_extra, [pltpu.VMEM((t // blk, HEAD_DIM, blk), BF16)],
            layer=l, blk=blk,
            col_q=c_qd // HEAD_DIM, col_g=c_gd // HEAD_DIM, name="diff_prompt")
        o_s = _prompt_attention(
            functools.partial(_sb_prompt_kernel, blk=blk), proj3, stacks[2], stacks[3], [], [],
            layer=l, blk=2 * blk, col_q=c_qs // HEAD_DIM, col_g=c_gs // HEAD_DIM, name="sb_prompt")
        xp = _outproj(y2, proj, o_d.reshape(bp * t, w_diff), o_s.reshape(bp * t, w_sb), xp,
                      wg, w_a, w_d, w_s, tm=tmo_p, ga_block=c_ga // w_ssm)
        ht = ht.reshape(bp, nch, 2, SSM_CHUNK, n)
        ssm_p.append((ht[:, :, 0].reshape(bp, g, n), ht[:, :, 1].reshape(bp, g, n)))

        proj_s = _proj(xs, ng, w_a_in, q_gain, pmat, q_cols=(c_qd, c_gd), tm=bs, tn=tn)
        kv_s = _kvproj_step(xs, ng, w_b_in, k_gain, pmat)
        proj_s3 = proj_s.reshape(bs, 1, cols_a)
        kv_s3 = kv_s.reshape(bs, 1, 4 * w_diff)
        h0 = jnp.concatenate([state_ssm_re[l].reshape(bs, nch, half),
                              state_ssm_im[l].reshape(bs, nch, half)], axis=-1)
        y2s, hts = _s5_step(proj_s, jnp.swapaxes(h0, 0, 1), bbd, cbd, ar_l, ai_l, d_skip)
        od_s = _decode_attention(
            functools.partial(_diff_decode_kernel, n_pp=n_pp, col_q=c_qd, col_g=c_gd, n_heads=dh,
                              lam_init=lam_init),
            page_table, [proj_s3, kv_s3[:, :, :2 * w_diff]], ck_d, cv_d, diff_extra,
            [pltpu.VMEM((dh, SUBLANES, HEAD_DIM), BF16), pltpu.VMEM((dh * SUBLANES, 1), F32),
             pltpu.VMEM((dh * SUBLANES, 1), F32), pltpu.VMEM((dh * SUBLANES, HEAD_DIM), F32)],
            layer=l, n_pp=n_pp, reverse=False, name="diff_decode")
        os_s = _sb_decode(page_table, proj_s3, ck_s, cv_s, layer=l, n_pp=n_pp_sb, col_q=c_qs, col_g=c_gs)
        xs = _outproj(y2s, proj_s, od_s.reshape(bs, w_diff), os_s.reshape(bs, w_sb), xs,
                      wg, w_a, w_d, w_s, tm=bs, ga_block=c_ga // w_ssm)
        hts = jnp.swapaxes(hts, 0, 1).reshape(bs, nch, 2, SSM_CHUNK, n)
        kv4 = kv_s.reshape(bs, 1, 4, dh, HEAD_DIM)
        rows_s.append((kv4[:, :, 0], kv4[:, :, 1], kv4[:, :, 2], kv4[:, :, 3],
                       hts[:, :, 0].reshape(bs, g, n), hts[:, :, 1].reshape(bs, g, n)))

    kv_p = [jnp.transpose(s, (0, 1, 3, 2, 4)) for s in stacks]
    ssm_p = [jnp.stack(r) for r in zip(*ssm_p)]
    outs_s = [jnp.stack(r) for r in zip(*rows_s)]
    return (xp.reshape(bp, t, d), xs.reshape(bs, 1, d), *kv_p, *ssm_p, *outs_s)
```
